```python
import math
import jax, jax.numpy as jnp
from jax import lax
import numpy as np

D_MODEL = 2048
BATCH = 1
SEQ = 8192
DEPTH = 4

N_MIXERS = 3
N_SB_LAYERS = (DEPTH + 2) // 3
N_CF_LAYERS = (DEPTH + 1) // 3
N_M2_LAYERS = DEPTH // 3
SB_HEADS = 16
SB_HEAD_DIM = D_MODEL // SB_HEADS
SB_BLOCK = 128
CF_KERNEL = 31
M2_EXPAND = 2
M2_D_INNER = M2_EXPAND * D_MODEL
M2_HEAD_DIM = 64
M2_HEADS = M2_D_INNER // M2_HEAD_DIM
M2_GROUPS = 8
M2_HEADS_PER_GROUP = M2_HEADS // M2_GROUPS
M2_STATE = 128
M2_CONV = 4
M2_CHUNK = 128
M2_CONV_DIM = M2_D_INNER + 2 * M2_GROUPS * M2_STATE
M2_IN_DIM = 2 * M2_D_INNER + 2 * M2_GROUPS * M2_STATE + M2_HEADS
MLP_HIDDEN = 4 * D_MODEL
RMS_EPS = 1e-6
LN_EPS = 1e-5

kernel_name = "hybrid_sb_conformer_ssd_trunk"


def rms_norm(x, w, eps=RMS_EPS):
    xf = x.astype(jnp.float32)
    y = xf * lax.rsqrt(jnp.mean(xf * xf, axis=-1, keepdims=True) + eps)
    return (y * w.astype(jnp.float32)).astype(x.dtype)


def layer_norm(x, w, b, eps=LN_EPS):
    xf = x.astype(jnp.float32)
    mu = jnp.mean(xf, axis=-1, keepdims=True)
    xc = xf - mu
    y = xc * lax.rsqrt(jnp.mean(xc * xc, axis=-1, keepdims=True) + eps)
    return (y * w.astype(jnp.float32) + b.astype(jnp.float32)).astype(x.dtype)


def causal_depthwise_conv(x, w, b):
    k_width, ch = w.shape
    y = lax.conv_general_dilated(
        x, w[:, None, :].astype(x.dtype), window_strides=(1,),
        padding=[(k_width - 1, 0)], dimension_numbers=('NWC', 'WIO', 'NWC'),
        feature_group_count=ch)
    return y + b.astype(x.dtype)


def stick_breaking_mixer(h, w_qkv, q_norm_w, k_norm_w, w_o):
    b, l, _ = h.shape
    qkv = (h @ w_qkv).reshape(b, l, 3, SB_HEADS, SB_HEAD_DIM)
    q = rms_norm(qkv[:, :, 0], q_norm_w).transpose(0, 2, 1, 3)
    k = rms_norm(qkv[:, :, 1], k_norm_w).transpose(0, 2, 1, 3)
    v = qkv[:, :, 2].transpose(0, 2, 1, 3)
    n_blk = l // SB_BLOCK
    q_blocks = q.reshape(b, SB_HEADS, n_blk, SB_BLOCK, SB_HEAD_DIM).transpose(2, 0, 1, 3, 4)
    key_pos = jnp.arange(l)
    scale = 1.0 / math.sqrt(SB_HEAD_DIM)

    def block(args):
        i, qi = args
        z = jnp.einsum('bhqd,bhkd->bhqk', qi, k).astype(jnp.float32) * scale
        q_pos = i * SB_BLOCK + jnp.arange(SB_BLOCK)
        strict = key_pos[None, :] < q_pos[:, None]
        log_keep = jnp.where(strict, -jax.nn.softplus(z), 0.0)
        later = lax.cumsum(log_keep, axis=3, reverse=True) - log_keep
        att = jnp.where(strict, jnp.exp(jax.nn.log_sigmoid(z) + later), 0.0)
        return jnp.einsum('bhqk,bhkd->bhqd', att.astype(v.dtype), v)

    o = lax.map(block, (jnp.arange(n_blk), q_blocks))
    o = o.transpose(1, 0, 3, 2, 4).reshape(b, l, D_MODEL)
    return o @ w_o


def conformer_conv_mixer(h, w_in, b_in, dw_w, dw_b, ln_w, ln_b, w_out, b_out):
    u = h @ w_in + b_in
    val, gate = jnp.split(u, 2, axis=-1)
    u = val * jax.nn.sigmoid(gate)
    u = causal_depthwise_conv(u, dw_w, dw_b)
    u = jax.nn.silu(layer_norm(u, ln_w, ln_b))
    return u @ w_out + b_out


def ssd_chunked(xs, dt, a, bm, cm):
    b, l, g, r, p = xs.shape
    n = bm.shape[-1]
    qn = M2_CHUNK
    c = l // qn
    xdt = (xs * dt[..., None]).reshape(b, c, qn, g, r, p)
    bc = bm.reshape(b, c, qn, g, n)
    cc = cm.reshape(b, c, qn, g, n)
    a_cum = jnp.cumsum((dt * a).reshape(b, c, qn, g, r), axis=2)
    causal = jnp.tril(jnp.ones((qn, qn), dtype=bool))[None, None, :, :, None, None]
    seg = a_cum[:, :, :, None] - a_cum[:, :, None, :]
    decay_in = jnp.exp(jnp.where(causal, seg, -jnp.inf))
    cb = jnp.einsum('bcqgn,bcsgn->bcqsg', cc, bc)
    y_diag = jnp.einsum('bcqsg,bcqsgr,bcsgrp->bcqgrp', cb, decay_in, xdt)
    decay_out = jnp.exp(a_cum[:, :, -1:] - a_cum)
    states = jnp.einsum('bcsgn,bcsgr,bcsgrp->bcgrpn', bc, decay_out, xdt)
    chunk_decay = jnp.exp(a_cum[:, :, -1])

    def step(h_state, inp):
        st, dec = inp
        return h_state * dec[..., None, None] + st, h_state

    h0 = jnp.zeros((b, g, r, p, n), jnp.float32)
    _, prev = lax.scan(step, h0, (jnp.moveaxis(states, 1, 0), jnp.moveaxis(chunk_decay, 1, 0)))
    prev = jnp.moveaxis(prev, 0, 1)
    y_off = jnp.einsum('bcqgn,bcgrpn,bcqgr->bcqgrp', cc, prev, jnp.exp(a_cum))
    return (y_diag + y_off).reshape(b, l, g, r, p)


def mamba2_mixer(h, w_in, conv_w, conv_b, dt_bias, a_log, d_skip, norm_w, w_out):
    b, l, _ = h.shape
    f32 = jnp.float32
    zxbcdt = h @ w_in
    z, xbc, dt = jnp.split(zxbcdt, [M2_D_INNER, M2_D_INNER + M2_CONV_DIM], axis=-1)
    xbc = jax.nn.silu(causal_depthwise_conv(xbc, conv_w, conv_b))
    xs, bm, cm = jnp.split(xbc, [M2_D_INNER, M2_D_INNER + M2_GROUPS * M2_STATE], axis=-1)
    xs = xs.reshape(b, l, M2_GROUPS, M2_HEADS_PER_GROUP, M2_HEAD_DIM).astype(f32)
    bm = bm.reshape(b, l, M2_GROUPS, M2_STATE).astype(f32)
    cm = cm.reshape(b, l, M2_GROUPS, M2_STATE).astype(f32)
    dt = jax.nn.softplus(dt.astype(f32) + dt_bias.astype(f32)).reshape(b, l, M2_GROUPS, M2_HEADS_PER_GROUP)
    a = -jnp.exp(a_log.astype(f32)).reshape(M2_GROUPS, M2_HEADS_PER_GROUP)
    d = d_skip.astype(f32).reshape(M2_GROUPS, M2_HEADS_PER_GROUP)[..., None]
    y = ssd_chunked(xs, dt, a, bm, cm) + d * xs
    group_w = M2_HEADS_PER_GROUP * M2_HEAD_DIM
    y = y.reshape(b, l, M2_GROUPS, group_w) * jax.nn.silu(z.astype(f32)).reshape(b, l, M2_GROUPS, group_w)
    y = rms_norm(y, norm_w.reshape(M2_GROUPS, group_w), eps=1e-5)
    return y.reshape(b, l, M2_D_INNER).astype(h.dtype) @ w_out


def squared_relu_mlp(h, w_up, w_down):
    return jnp.square(jax.nn.relu(h @ w_up)) @ w_down


def setup_inputs(seed: int = 0) -> dict:
    key = jax.random.key(seed)
    ks = iter(jax.random.split(key, 32))
    f32 = jnp.float32

    def nrm(shape, fan_in):
        return jax.random.normal(next(ks), shape, f32) * (fan_in ** -0.5)

    def gain(shape):
        return 1.0 + 0.02 * jax.random.normal(next(ks), shape, f32)

    def bias(shape):
        return 0.01 * jax.random.normal(next(ks), shape, f32)

    x = jax.random.normal(next(ks), (BATCH, SEQ, D_MODEL), f32)
    dt0 = jnp.exp(jax.random.uniform(next(ks), (N_M2_LAYERS, M2_HEADS), f32,
                                     minval=math.log(1e-3), maxval=math.log(1e-1)))
    return {
        "x": x,
        "norm_mix_w": gain((DEPTH, D_MODEL)),
        "norm_mlp_w": gain((DEPTH, D_MODEL)),
        "sb_w_qkv": nrm((N_SB_LAYERS, D_MODEL, 3 * D_MODEL), D_MODEL),
        "sb_q_norm_w": gain((N_SB_LAYERS, SB_HEAD_DIM)),
        "sb_k_norm_w": gain((N_SB_LAYERS, SB_HEAD_DIM)),
        "sb_w_o": nrm((N_SB_LAYERS, D_MODEL, D_MODEL), D_MODEL),
        "cf_w_in": nrm((N_CF_LAYERS, D_MODEL, 2 * D_MODEL), D_MODEL),
        "cf_b_in": bias((N_CF_LAYERS, 2 * D_MODEL)),
        "cf_dw_w": nrm((N_CF_LAYERS, CF_KERNEL, D_MODEL), CF_KERNEL),
        "cf_dw_b": bias((N_CF_LAYERS, D_MODEL)),
        "cf_ln_w": gain((N_CF_LAYERS, D_MODEL)),
        "cf_ln_b": bias((N_CF_LAYERS, D_MODEL)),
        "cf_w_out": nrm((N_CF_LAYERS, D_MODEL, D_MODEL), D_MODEL),
        "cf_b_out": bias((N_CF_LAYERS, D_MODEL)),
        "m2_w_in": nrm((N_M2_LAYERS, D_MODEL, M2_IN_DIM), D_MODEL),
        "m2_conv_w": nrm((N_M2_LAYERS, M2_CONV, M2_CONV_DIM), M2_CONV),
        "m2_conv_b": bias((N_M2_LAYERS, M2_CONV_DIM)),
        "m2_dt_bias": dt0 + jnp.log(-jnp.expm1(-dt0)),
        "m2_a_log": jnp.log(jax.random.uniform(next(ks), (N_M2_LAYERS, M2_HEADS), f32, minval=1.0, maxval=16.0)),
        "m2_d": gain((N_M2_LAYERS, M2_HEADS)),
        "m2_norm_w": gain((N_M2_LAYERS, M2_D_INNER)),
        "m2_w_out": nrm((N_M2_LAYERS, M2_D_INNER, D_MODEL), M2_D_INNER),
        "mlp_w_up": nrm((DEPTH, D_MODEL, MLP_HIDDEN), D_MODEL),
        "mlp_w_down": nrm((DEPTH, MLP_HIDDEN, D_MODEL), MLP_HIDDEN),
    }


def reference(x, norm_mix_w, norm_mlp_w, sb_w_qkv, sb_q_norm_w, sb_k_norm_w, sb_w_o,
              cf_w_in, cf_b_in, cf_dw_w, cf_dw_b, cf_ln_w, cf_ln_b, cf_w_out, cf_b_out,
              m2_w_in, m2_conv_w, m2_conv_b, m2_dt_bias, m2_a_log, m2_d, m2_norm_w, m2_w_out,
              mlp_w_up, mlp_w_down):
    for i in range(DEPTH):
        kind, j = i % N_MIXERS, i // N_MIXERS
        h = rms_norm(x, norm_mix_w[i])
        if kind == 0:
            mix = stick_breaking_mixer(h, sb_w_qkv[j], sb_q_norm_w[j], sb_k_norm_w[j], sb_w_o[j])
        elif kind == 1:
            mix = conformer_conv_mixer(h, cf_w_in[j], cf_b_in[j], cf_dw_w[j], cf_dw_b[j],
                                       cf_ln_w[j], cf_ln_b[j], cf_w_out[j], cf_b_out[j])
        else:
            mix = mamba2_mixer(h, m2_w_in[j], m2_conv_w[j], m2_conv_b[j], m2_dt_bias[j],
                               m2_a_log[j], m2_d[j], m2_norm_w[j], m2_w_out[j])
        x = x + mix.astype(x.dtype)
        h = rms_norm(x, norm_mlp_w[i])
        x = x + squared_relu_mlp(h, mlp_w_up[i], mlp_w_down[i]).astype(x.dtype)
    return x
```

```python
import functools
import math

import jax
import jax.numpy as jnp
from jax import lax
from jax.experimental import pallas as pl
from jax.experimental.pallas import tpu as pltpu

F32 = jnp.float32
BF16 = jnp.bfloat16

N_MIXERS = 3
SB_BLOCK_KEYS = 128
M2_GROUPS = 8
M2_STATE = 128
M2_CHUNK = 128
RMS_EPS = 1e-6
LN_EPS = 1e-5
M2_NORM_EPS = 1e-5

V7X_LANES = 128
V7X_VMEM_BYTES = 64 * 1024 * 1024
VMEM_LIMIT_BYTES = V7X_VMEM_BYTES - 8 * 1024 * 1024


def _params(*sem):
    return pltpu.CompilerParams(dimension_semantics=sem, vmem_limit_bytes=VMEM_LIMIT_BYTES)


def _row(v):
    return v.reshape(1, -1)


def _rms_norm_to(x_ref, nw_ref, xn_ref, eps, rows=64):
    def body(r, carry):
        r0 = pl.multiple_of(r * rows, rows)
        x = x_ref[pl.ds(r0, rows), :]
        ms = jnp.mean(x * x, axis=-1, keepdims=True)
        xn_ref[pl.ds(r0, rows), :] = (x * lax.rsqrt(ms + eps) * nw_ref[...]).astype(xn_ref.dtype)
        return carry
    lax.fori_loop(0, x_ref.shape[0] // rows, body, 0)


def _split_bf16(x):
    hi = x.astype(BF16)
    lo = (x - hi.astype(F32)).astype(BF16)
    return hi, lo


def _dot(a, b):
    return jnp.dot(a, b, preferred_element_type=F32)


def _silu(x):
    return x * (1.0 / (1.0 + jnp.exp(-x)))


def _softplus(x):
    return jnp.maximum(x, 0.0) + jnp.log(1.0 + jnp.exp(-jnp.abs(x)))


def _nmm_plain_kernel(x_ref, nw_ref, w_ref, o_ref, xn_ref):
    @pl.when(pl.program_id(1) == 0)
    def _():
        _rms_norm_to(x_ref, nw_ref, xn_ref, RMS_EPS)
    o_ref[...] = _dot(xn_ref[...], w_ref[...]).astype(o_ref.dtype)


def _nmm_qkv_kernel(x_ref, nw_ref, w_ref, g_ref, o_ref, xn_ref, *, n_norm_tiles, hd):
    j = pl.program_id(1)

    @pl.when(j == 0)
    def _():
        _rms_norm_to(x_ref, nw_ref, xn_ref, RMS_EPS)

    acc = _dot(xn_ref[...], w_ref[...])

    @pl.when(j < n_norm_tiles)
    def _():
        for c in range(acc.shape[1] // hd):
            cols = slice(c * hd, (c + 1) * hd)
            blk = acc[:, cols]
            ms = jnp.mean(blk * blk, axis=-1, keepdims=True)
            o_ref[:, cols] = (blk * lax.rsqrt(ms + RMS_EPS) * g_ref[:, cols]).astype(o_ref.dtype)

    @pl.when(j >= n_norm_tiles)
    def _():
        o_ref[...] = acc.astype(o_ref.dtype)


def _nmm_glu_kernel(x_ref, nw_ref, wv_ref, wg_ref, bv_ref, bg_ref, o_ref, xn_ref):
    @pl.when(pl.program_id(1) == 0)
    def _():
        _rms_norm_to(x_ref, nw_ref, xn_ref, RMS_EPS)
    xn = xn_ref[...]
    val = _dot(xn, wv_ref[...]) + bv_ref[...]
    gate = _dot(xn, wg_ref[...]) + bg_ref[...]
    o_ref[...] = (val * (1.0 / (1.0 + jnp.exp(-gate)))).astype(o_ref.dtype)


def _norm_matmul(x, nw, w, *, tm, tn, out_dtype):
    l, d = x.shape
    n = w.shape[1]
    return pl.pallas_call(
        _nmm_plain_kernel,
        out_shape=jax.ShapeDtypeStruct((l, n), out_dtype),
        grid=(l // tm, n // tn),
        in_specs=[pl.BlockSpec((tm, d), lambda i, j: (i, 0)),
                  pl.BlockSpec((1, d), lambda i, j: (0, 0)),
                  pl.BlockSpec((d, tn), lambda i, j: (0, j))],
        out_specs=pl.BlockSpec((tm, tn), lambda i, j: (i, j)),
        scratch_shapes=[pltpu.VMEM((tm, d), BF16)],
        compiler_params=_params("parallel", "arbitrary"),
        name="norm_matmul",
    )(x, _row(nw), w)


def _norm_matmul_qkv(x, nw, w, gains, *, n_norm_cols, hd, tm, tn):
    l, d = x.shape
    n = w.shape[1]
    kern = functools.partial(_nmm_qkv_kernel, n_norm_tiles=n_norm_cols // tn, hd=hd)
    return pl.pallas_call(
        kern,
        out_shape=jax.ShapeDtypeStruct((l, n), BF16),
        grid=(l // tm, n // tn),
        in_specs=[pl.BlockSpec((tm, d), lambda i, j: (i, 0)),
                  pl.BlockSpec((1, d), lambda i, j: (0, 0)),
                  pl.BlockSpec((d, tn), lambda i, j: (0, j)),
                  pl.BlockSpec((1, tn), lambda i, j: (0, j))],
        out_specs=pl.BlockSpec((tm, tn), lambda i, j: (i, j)),
        scratch_shapes=[pltpu.VMEM((tm, d), BF16)],
        compiler_params=_params("parallel", "arbitrary"),
        name="norm_matmul_qkv",
    )(x, _row(nw), w, _row(gains))


def _norm_matmul_glu(x, nw, w, b, *, tm, tn):
    l, d = x.shape
    n = w.shape[1] // 2
    nt = n // tn
    b = _row(b)
    return pl.pallas_call(
        _nmm_glu_kernel,
        out_shape=jax.ShapeDtypeStruct((l, n), F32),
        grid=(l // tm, nt),
        in_specs=[pl.BlockSpec((tm, d), lambda i, j: (i, 0)),
                  pl.BlockSpec((1, d), lambda i, j: (0, 0)),
                  pl.BlockSpec((d, tn), lambda i, j: (0, j)),
                  pl.BlockSpec((d, tn), lambda i, j: (0, j + nt)),
                  pl.BlockSpec((1, tn), lambda i, j: (0, j)),
                  pl.BlockSpec((1, tn), lambda i, j: (0, j + nt))],
        out_specs=pl.BlockSpec((tm, tn), lambda i, j: (i, j)),
        scratch_shapes=[pltpu.VMEM((tm, d), BF16)],
        compiler_params=_params("parallel", "arbitrary"),
        name="norm_matmul_glu",
    )(x, _row(nw), w, w, b, b)


def _mm_res_kernel(a_ref, w_ref, r_ref, o_ref):
    o_ref[...] = r_ref[...] + _dot(a_ref[...], w_ref[...])


def _matmul_residual(a, w, res, *, tm, tn):
    l, k = a.shape
    n = w.shape[1]
    return pl.pallas_call(
        _mm_res_kernel,
        out_shape=jax.ShapeDtypeStruct((l, n), F32),
        grid=(l // tm, n // tn),
        in_specs=[pl.BlockSpec((tm, k), lambda i, j: (i, 0)),
                  pl.BlockSpec((k, tn), lambda i, j: (0, j)),
                  pl.BlockSpec((tm, tn), lambda i, j: (i, j))],
        out_specs=pl.BlockSpec((tm, tn), lambda i, j: (i, j)),
        compiler_params=_params("parallel", "arbitrary"),
        name="matmul_residual",
    )(a, w, res)


def _mlp_kernel(x_ref, nw_ref, wu_ref, wd_ref, o_ref, xn_ref):
    j = pl.program_id(1)

    @pl.when(j == 0)
    def _():
        _rms_norm_to(x_ref, nw_ref, xn_ref, RMS_EPS)

    h = jnp.maximum(_dot(xn_ref[...], wu_ref[...]), 0.0)
    contrib = _dot((h * h).astype(BF16), wd_ref[...])

    @pl.when(j == 0)
    def _():
        o_ref[...] = x_ref[...] + contrib

    @pl.when(j > 0)
    def _():
        o_ref[...] += contrib


def _mlp(x, nw, w_up, w_down, *, tm, th):
    l, d = x.shape
    hid = w_up.shape[1]
    return pl.pallas_call(
        _mlp_kernel,
        out_shape=jax.ShapeDtypeStruct((l, d), F32),
        grid=(l // tm, hid // th),
        in_specs=[pl.BlockSpec((tm, d), lambda i, j: (i, 0)),
                  pl.BlockSpec((1, d), lambda i, j: (0, 0)),
                  pl.BlockSpec((d, th), lambda i, j: (0, j)),
                  pl.BlockSpec((th, d), lambda i, j: (j, 0))],
        out_specs=pl.BlockSpec((tm, d), lambda i, j: (i, 0)),
        scratch_shapes=[pltpu.VMEM((tm, d), BF16)],
        compiler_params=_params("parallel", "arbitrary"),
        name="mlp",
    )(x, _row(nw), w_up, w_down)


def _sb_attn_kernel(q_ref, k_ref, v_ref, u_ref, o_ref, acc_ref, c_ref, *, tq, ck):
    i = pl.program_id(1)
    nsub = tq // ck
    acc_ref[...] = jnp.zeros_like(acc_ref)
    c_ref[...] = jnp.zeros_like(c_ref)

    def tile(rows, start, masked):
        q = q_ref[rows, :]
        kb = k_ref[pl.ds(start, ck), :]
        vb = v_ref[pl.ds(start, ck), :]
        z = lax.dot_general(q, kb, (((1,), (1,)), ((), ())), preferred_element_type=F32)
        l = jnp.log(1.0 + jnp.exp(-jnp.abs(z)))
        log_keep = -(jnp.maximum(z, 0.0) + l)
        log_beta = jnp.minimum(z, 0.0) - l
        if masked:
            row_id = lax.broadcasted_iota(jnp.int32, z.shape, 0)
            col_id = lax.broadcasted_iota(jnp.int32, z.shape, 1)
            strict = col_id < row_id
            log_keep = jnp.where(strict, log_keep, 0.0)
        r = _dot(log_keep.astype(BF16), u_ref[...])
        att = jnp.exp(log_beta + r[:, :ck] + c_ref[rows, :])
        if masked:
            att = jnp.where(strict, att, 0.0)
        acc_ref[rows, :] += _dot(att.astype(BF16), vb)
        c_ref[rows, :] += r[:, ck:]

    for m in reversed(range(nsub)):
        start = pl.multiple_of(i * tq + m * ck, ck)
        tile(slice(m * ck, tq), start, True)

    def block(t, carry):
        base = (i - 1 - t) * tq
        for m in reversed(range(nsub)):
            tile(slice(0, tq), pl.multiple_of(base + m * ck, ck), False)
        return carry
    lax.fori_loop(0, i, block, 0)

    o_ref[...] = acc_ref[...].astype(o_ref.dtype)


def _sb_attention(qkv, *, heads, hd, tq):
    l = qkv.shape[0]
    ck = SB_BLOCK_KEYS
    j_id = lax.broadcasted_iota(jnp.int32, (ck, 2 * ck), 0)
    s_id = lax.broadcasted_iota(jnp.int32, (ck, 2 * ck), 1)
    u = ((j_id > s_id) | (s_id >= ck)).astype(BF16)
    kern = functools.partial(_sb_attn_kernel, tq=tq, ck=ck)
    return pl.pallas_call(
        kern,
        out_shape=jax.ShapeDtypeStruct((l, heads * hd), BF16),
        grid=(heads, l // tq),
        in_specs=[pl.BlockSpec((tq, hd), lambda h, i: (i, h)),
                  pl.BlockSpec((l, hd), lambda h, i: (0, heads + h)),
                  pl.BlockSpec((l, hd), lambda h, i: (0, 2 * heads + h)),
                  pl.BlockSpec((ck, 2 * ck), lambda h, i: (0, 0))],
        out_specs=pl.BlockSpec((tq, hd), lambda h, i: (i, h)),
        scratch_shapes=[pltpu.VMEM((tq, hd), F32), pltpu.VMEM((tq, ck), F32)],
        compiler_params=_params("parallel", "arbitrary"),
        name="sb_attention",
    )(qkv, qkv, qkv, u)


def _cf_conv_kernel(u_ref, halo_ref, dw_ref, dwb_ref, lnw_ref, lnb_ref, wo_ref, bo_ref, x_ref,
                    o_ref, ext_ref, y_ref, *, taps, halo, rows):
    i = pl.program_id(0)
    tl, ch = u_ref.shape

    @pl.when(i == 0)
    def _():
        ext_ref[0:halo, :] = jnp.zeros((halo, ch), F32)

    @pl.when(i > 0)
    def _():
        ext_ref[0:halo, :] = halo_ref[...]

    ext_ref[halo:, :] = u_ref[...]

    off = halo - (taps - 1)

    def lane_block(c, carry):
        c0 = pl.multiple_of(c * V7X_LANES, V7X_LANES)
        lanes = pl.ds(c0, V7X_LANES)
        for r0 in range(0, tl, rows):
            acc = jnp.zeros((rows, V7X_LANES), F32) + dwb_ref[:, lanes]
            for k in range(taps):
                acc = acc + dw_ref[k:k + 1, lanes] * ext_ref[r0 + off + k:r0 + off + k + rows, lanes]
            y_ref[r0:r0 + rows, lanes] = acc
        return carry
    lax.fori_loop(0, ch // V7X_LANES, lane_block, 0)

    y = y_ref[...]
    mu = jnp.mean(y, axis=-1, keepdims=True)
    yc = y - mu
    var = jnp.mean(yc * yc, axis=-1, keepdims=True)
    v = yc * lax.rsqrt(var + LN_EPS) * lnw_ref[...] + lnb_ref[...]
    v = _silu(v).astype(BF16)
    o_ref[...] = x_ref[...] + _dot(v, wo_ref[...]) + bo_ref[...]


def _conformer_conv_out(u, dw_w, dw_b, ln_w, ln_b, w_out, b_out, x, *, tl):
    l, ch = u.shape
    taps = dw_w.shape[0]
    halo = 32
    assert taps - 1 <= halo and tl % halo == 0
    kern = functools.partial(_cf_conv_kernel, taps=taps, halo=halo, rows=64)
    hb = tl // halo
    return pl.pallas_call(
        kern,
        out_shape=jax.ShapeDtypeStruct((l, ch), F32),
        grid=(l // tl,),
        in_specs=[pl.BlockSpec((tl, ch), lambda i: (i, 0)),
                  pl.BlockSpec((halo, ch), lambda i: (jnp.maximum(i * hb - 1, 0), 0)),
                  pl.BlockSpec((taps, ch), lambda i: (0, 0)),
                  pl.BlockSpec((1, ch), lambda i: (0, 0)),
                  pl.BlockSpec((1, ch), lambda i: (0, 0)),
                  pl.BlockSpec((1, ch), lambda i: (0, 0)),
                  pl.BlockSpec((ch, ch), lambda i: (0, 0)),
                  pl.BlockSpec((1, ch), lambda i: (0, 0)),
                  pl.BlockSpec((tl, ch), lambda i: (i, 0))],
        out_specs=pl.BlockSpec((tl, ch), lambda i: (i, 0)),
        scratch_shapes=[pltpu.VMEM((tl + halo, ch), F32), pltpu.VMEM((tl, ch), F32)],
        compiler_params=_params("arbitrary"),
        name="conformer_conv_out",
    )(u, u, dw_w, _row(dw_b), _row(ln_w), _row(ln_b), w_out, _row(b_out), x)


def _m2_conv_kernel(u_ref, halo_ref, w_ref, b_ref, o_ref, ext_ref, *, taps, halo):
    i = pl.program_id(0)
    tl, tc = u_ref.shape

    @pl.when(i == 0)
    def _():
        ext_ref[0:halo, :] = jnp.zeros((halo, tc), F32)

    @pl.when(i > 0)
    def _():
        ext_ref[0:halo, :] = halo_ref[...]

    ext_ref[halo:, :] = u_ref[...]
    off = halo - (taps - 1)
    acc = jnp.zeros((tl, tc), F32) + b_ref[...]
    for k in range(taps):
        acc = acc + w_ref[k:k + 1, :] * ext_ref[off + k:off + k + tl, :]
    o_ref[...] = _silu(acc)


def _m2_conv(zx, conv_w, conv_b, *, col0, tl, tc):
    l = zx.shape[0]
    taps, c = conv_w.shape
    halo = 8
    kern = functools.partial(_m2_conv_kernel, taps=taps, halo=halo)
    hb = tl // halo
    cb0 = col0 // tc
    return pl.pallas_call(
        kern,
        out_shape=jax.ShapeDtypeStruct((l, c), F32),
        grid=(l // tl, c // tc),
        in_specs=[pl.BlockSpec((tl, tc), lambda i, j: (i, cb0 + j)),
                  pl.BlockSpec((halo, tc), lambda i, j: (jnp.maximum(i * hb - 1, 0), cb0 + j)),
                  pl.BlockSpec((taps, tc), lambda i, j: (0, j)),
                  pl.BlockSpec((1, tc), lambda i, j: (0, j))],
        out_specs=pl.BlockSpec((tl, tc), lambda i, j: (i, j)),
        scratch_shapes=[pltpu.VMEM((tl + halo, tc), F32)],
        compiler_params=_params("arbitrary", "arbitrary"),
        name="m2_conv",
    )(zx, zx, conv_w, _row(conv_b))


def _ssd_kernel(z_ref, xs_ref, b_ref, c_ref, dt_ref, dtb_ref, e5_ref, e10_ref, al5_ref, al10_ref,
                d5_ref, nw_ref, tril_ref, o_ref, s_ref, *, heads_per_group, hp):
    ci = pl.program_id(1)
    q = xs_ref.shape[0]

    @pl.when(ci == 0)
    def _():
        s_ref[...] = jnp.zeros_like(s_ref)

    dt = _softplus(dt_ref[...] + dtb_ref[...])
    dt_hi, dt_lo = _split_bf16(dt)
    dt5 = _dot(dt_hi, e5_ref[...]) + _dot(dt_lo, e5_ref[...])
    dt10 = _dot(dt_hi, e10_ref[...]) + _dot(dt_lo, e10_ref[...])
    tril = tril_ref[...]
    a5 = -jnp.exp(al5_ref[...])
    a10 = -jnp.exp(al10_ref[...])
    da5_hi, da5_lo = _split_bf16(dt5 * a5)
    da10_hi, da10_lo = _split_bf16(dt10 * a10)
    acum5 = _dot(tril, da5_hi) + _dot(tril, da5_lo)
    acum10 = _dot(tril, da10_hi) + _dot(tril, da10_lo)

    xs = xs_ref[...]
    xdt = xs * dt5
    last = acum5[q - 1:q, :]
    xdec = (xdt * jnp.exp(last - acum5)).astype(BF16)
    bm = b_ref[...]
    cm = c_ref[...].astype(BF16)
    state = s_ref[...]

    y = _dot(cm, state.astype(BF16)) * jnp.exp(acum5)
    s_ref[...] = state * jnp.exp(last) + _dot(bm.T.astype(BF16), xdec)

    cb = lax.dot_general(cm, bm.astype(BF16), (((1,), (1,)), ((), ())), preferred_element_type=F32)
    row_id = lax.broadcasted_iota(jnp.int32, (q, q), 0)
    col_id = lax.broadcasted_iota(jnp.int32, (q, q), 1)
    causal = col_id <= row_id
    lane_id = lax.broadcasted_iota(jnp.int32, (q, 2 * hp), 1)
    diag = []
    for pair in range(heads_per_group // 2):
        xpair = xdt[:, pair * 2 * hp:(pair + 1) * 2 * hp]
        ypair = jnp.zeros((q, 2 * hp), F32)
        for half in range(2):
            r = 2 * pair + half
            col = acum10[:, r * q:(r + 1) * q]
            seg = col - col.T
            m = (cb * jnp.exp(jnp.where(causal, seg, -jnp.inf))).astype(BF16)
            mine = (lane_id >= half * hp) & (lane_id < (half + 1) * hp)
            ypair = ypair + _dot(m, jnp.where(mine, xpair, 0.0).astype(BF16))
        diag.append(ypair)
    y = y + jnp.concatenate(diag, axis=1) + d5_ref[...] * xs

    zg = z_ref[...]
    y = y * _silu(zg)
    ms = jnp.mean(y * y, axis=-1, keepdims=True)
    o_ref[...] = (y * lax.rsqrt(ms + M2_NORM_EPS) * nw_ref[...]).astype(o_ref.dtype)


def _ssd(zx, xbc, dt_bias, a_log, d_skip, norm_w, *, d_inner, heads):
    l = zx.shape[0]
    g = M2_GROUPS
    n = M2_STATE
    qn = M2_CHUNK
    hpg = heads // g
    hp = d_inner // heads
    gw = hpg * hp
    assert qn == V7X_LANES and n == V7X_LANES and 2 * hp == V7X_LANES and heads <= V7X_LANES
    conv_dim = xbc.shape[1]
    dt_blk = (d_inner + conv_dim) // V7X_LANES

    hid = lax.broadcasted_iota(jnp.int32, (g, V7X_LANES, gw), 1)
    gid = lax.broadcasted_iota(jnp.int32, (g, V7X_LANES, gw), 0)
    lid = lax.broadcasted_iota(jnp.int32, (g, V7X_LANES, gw), 2)
    e5 = (hid == gid * hpg + lid // hp).astype(BF16)
    hid = lax.broadcasted_iota(jnp.int32, (g, V7X_LANES, hpg * qn), 1)
    gid = lax.broadcasted_iota(jnp.int32, (g, V7X_LANES, hpg * qn), 0)
    lid = lax.broadcasted_iota(jnp.int32, (g, V7X_LANES, hpg * qn), 2)
    e10 = (hid == gid * hpg + lid // qn).astype(BF16)
    tril = (lax.broadcasted_iota(jnp.int32, (qn, qn), 1)
            <= lax.broadcasted_iota(jnp.int32, (qn, qn), 0)).astype(BF16)

    dtb = jnp.zeros((1, V7X_LANES), F32).at[0, :heads].set(dt_bias)
    al5 = jnp.repeat(a_log, hp).reshape(g, 1, gw)
    al10 = jnp.repeat(a_log, qn).reshape(g, 1, hpg * qn)
    d5 = jnp.repeat(d_skip, hp).reshape(g, 1, gw)
    nw = norm_w.reshape(g, 1, gw)

    kern = functools.partial(_ssd_kernel, heads_per_group=hpg, hp=hp)
    xs_blk0 = 0
    b_blk0 = d_inner // n
    c_blk0 = (d_inner + g * n) // n
    return pl.pallas_call(
        kern,
        out_shape=jax.ShapeDtypeStruct((l, d_inner), BF16),
        grid=(g, l // qn),
        in_specs=[pl.BlockSpec((qn, gw), lambda gi, ci: (ci, gi)),
                  pl.BlockSpec((qn, gw), lambda gi, ci: (ci, xs_blk0 + gi)),
                  pl.BlockSpec((qn, n), lambda gi, ci: (ci, b_blk0 + gi)),
                  pl.BlockSpec((qn, n), lambda gi, ci: (ci, c_blk0 + gi)),
                  pl.BlockSpec((qn, V7X_LANES), lambda gi, ci: (ci, dt_blk)),
                  pl.BlockSpec((1, V7X_LANES), lambda gi, ci: (0, 0)),
                  pl.BlockSpec((None, V7X_LANES, gw), lambda gi, ci: (gi, 0, 0)),
                  pl.BlockSpec((None, V7X_LANES, hpg * qn), lambda gi, ci: (gi, 0, 0)),
                  pl.BlockSpec((None, 1, gw), lambda gi, ci: (gi, 0, 0)),
                  pl.BlockSpec((None, 1, hpg * qn), lambda gi, ci: (gi, 0, 0)),
                  pl.BlockSpec((None, 1, gw), lambda gi, ci: (gi, 0, 0)),
                  pl.BlockSpec((None, 1, gw), lambda gi, ci: (gi, 0, 0)),
                  pl.BlockSpec((qn, qn), lambda gi, ci: (0, 0))],
        out_specs=pl.BlockSpec((qn, gw), lambda gi, ci: (ci, gi)),
        scratch_shapes=[pltpu.VMEM((n, gw), F32)],
        compiler_params=_params("parallel", "arbitrary"),
        name="ssd",
    )(zx, xbc, xbc, xbc, zx, dtb, e5, e10, al5, al10, d5, nw, tril)


def _tiles():
    return dict(tm=512, tn=1024, th=512, tq=256, tl_cf=256, tl_m2=512, tc_m2=512)


def _stick_breaking_layer(x, nw, w_qkv, q_gain, k_gain, w_o, t):
    d = x.shape[1]
    hd = q_gain.shape[0]
    heads = d // hd
    scale = 1.0 / math.sqrt(hd)
    gains = jnp.concatenate([jnp.tile(q_gain * scale, heads), jnp.tile(k_gain, heads),
                             jnp.ones((d,), F32)])
    qkv = _norm_matmul_qkv(x, nw, w_qkv.astype(BF16), gains, n_norm_cols=2 * d, hd=hd,
                           tm=t["tm"], tn=t["tn"])
    o = _sb_attention(qkv, heads=heads, hd=hd, tq=t["tq"])
    return _matmul_residual(o, w_o.astype(BF16), x, tm=t["tm"], tn=t["tn"])


def _conformer_layer(x, nw, w_in, b_in, dw_w, dw_b, ln_w, ln_b, w_out, b_out, t):
    u = _norm_matmul_glu(x, nw, w_in.astype(BF16), b_in, tm=t["tm"], tn=t["tn"])
    return _conformer_conv_out(u, dw_w, dw_b, ln_w, ln_b, w_out.astype(BF16), b_out, x,
                               tl=t["tl_cf"])


def _mamba2_layer(x, nw, w_in, conv_w, conv_b, dt_bias, a_log, d_skip, norm_w, w_out, t):
    d_inner = norm_w.shape[0]
    heads = a_log.shape[0]
    n_in = w_in.shape[1]
    n_pad = -(-n_in // (9 * V7X_LANES)) * (9 * V7X_LANES)
    w_pad = jnp.pad(w_in.astype(BF16), ((0, 0), (0, n_pad - n_in)))
    zx = _norm_matmul(x, nw, w_pad, tm=t["tm"], tn=9 * V7X_LANES, out_dtype=F32)
    xbc = _m2_conv(zx, conv_w, conv_b, col0=d_inner, tl=t["tl_m2"], tc=t["tc_m2"])
    y = _ssd(zx, xbc, dt_bias, a_log, d_skip, norm_w, d_inner=d_inner, heads=heads)
    return _matmul_residual(y, w_out.astype(BF16), x, tm=t["tm"], tn=t["tn"])


def kernel(x, norm_mix_w, norm_mlp_w, sb_w_qkv, sb_q_norm_w, sb_k_norm_w, sb_w_o, cf_w_in, cf_b_in, cf_dw_w, cf_dw_b, cf_ln_w, cf_ln_b, cf_w_out, cf_b_out, m2_w_in, m2_conv_w, m2_conv_b, m2_dt_bias, m2_a_log, m2_d, m2_norm_w, m2_w_out, mlp_w_up, mlp_w_down):
    b, l, d = x.shape
    depth = norm_mix_w.shape[0]
    t = _tiles()
    outs = []
    for bi in range(b):
        h = x[bi]
        for i in range(depth):
            kind, j = i % N_MIXERS, i // N_MIXERS
            if kind == 0:
                h = _stick_breaking_layer(h, norm_mix_w[i], sb_w_qkv[j], sb_q_norm_w[j],
                                          sb_k_norm_w[j], sb_w_o[j], t)
            elif kind == 1:
                h = _conformer_layer(h, norm_mix_w[i], cf_w_in[j], cf_b_in[j], cf_dw_w[j],
                                     cf_dw_b[j], cf_ln_w[j], cf_ln_b[j], cf_w_out[j],
                                     cf_b_out[j], t)
            else:
                h = _mamba2_layer(h, norm_mix_w[i], m2_w_in[j], m2_conv_w[j], m2_conv_b[j],
                                  m2_dt_bias[j], m2_a_log[j], m2_d[j], m2_norm_w[j],
                                  m2_w_out[j], t)
            h = _mlp(h, norm_mlp_w[i], mlp_w_up[i].astype(BF16), mlp_w_down[i].astype(BF16),
                     tm=t["tm"], th=t["th"])
        outs.append(h)
    return jnp.stack(outs)
```

```python
import functools
import math

import jax
import jax.numpy as jnp
from jax import lax
from jax.experimental import pallas as pl
from jax.experimental.pallas import tpu as pltpu

F32 = jnp.float32
BF16 = jnp.bfloat16

N_MIXERS = 3
SB_BLOCK_KEYS = 256
LOG2_E = math.log2(math.e)
M2_GROUPS = 8
M2_STATE = 128
M2_CHUNK = 128
RMS_EPS = 1e-6
LN_EPS = 1e-5
M2_NORM_EPS = 1e-5

V7X_LANES = 128
V7X_VMEM_BYTES = 64 * 1024 * 1024
VMEM_LIMIT_BYTES = V7X_VMEM_BYTES - 8 * 1024 * 1024


def _params(*sem):
    return pltpu.CompilerParams(dimension_semantics=sem, vmem_limit_bytes=VMEM_LIMIT_BYTES)


def _row(v):
    return v.reshape(1, -1)


def _rms_norm_to(x_ref, nw_ref, xn_ref, eps, rows=64):
    def body(r, carry):
        r0 = pl.multiple_of(r * rows, rows)
        x = x_ref[pl.ds(r0, rows), :]
        ms = jnp.mean(x * x, axis=-1, keepdims=True)
        xn_ref[pl.ds(r0, rows), :] = (x * lax.rsqrt(ms + eps) * nw_ref[...]).astype(xn_ref.dtype)
        return carry
    lax.fori_loop(0, x_ref.shape[0] // rows, body, 0)


def _split_bf16(x):
    hi = x.astype(BF16)
    lo = (x - hi.astype(F32)).astype(BF16)
    return hi, lo


def _dot(a, b):
    return jnp.dot(a, b, preferred_element_type=F32)


def _silu(x):
    return x * (1.0 / (1.0 + jnp.exp(-x)))


def _softplus(x):
    return jnp.maximum(x, 0.0) + jnp.log(1.0 + jnp.exp(-jnp.abs(x)))


def _nmm_plain_kernel(x_ref, nw_ref, w_ref, o_ref, xn_ref):
    @pl.when(pl.program_id(1) == 0)
    def _():
        _rms_norm_to(x_ref, nw_ref, xn_ref, RMS_EPS)
    o_ref[...] = _dot(xn_ref[...], w_ref[...]).astype(o_ref.dtype)


def _nmm_qkv_kernel(x_ref, nw_ref, w_ref, g_ref, o_ref, xn_ref, *, n_norm_tiles, hd):
    j = pl.program_id(1)

    @pl.when(j == 0)
    def _():
        _rms_norm_to(x_ref, nw_ref, xn_ref, RMS_EPS)

    acc = _dot(xn_ref[...], w_ref[...])

    @pl.when(j < n_norm_tiles)
    def _():
        for c in range(acc.shape[1] // hd):
            cols = slice(c * hd, (c + 1) * hd)
            blk = acc[:, cols]
            ms = jnp.mean(blk * blk, axis=-1, keepdims=True)
            o_ref[:, cols] = (blk * lax.rsqrt(ms + RMS_EPS) * g_ref[:, cols]).astype(o_ref.dtype)

    @pl.when(j >= n_norm_tiles)
    def _():
        o_ref[...] = acc.astype(o_ref.dtype)


def _nmm_glu_kernel(x_ref, nw_ref, wv_ref, wg_ref, bv_ref, bg_ref, o_ref, xn_ref):
    @pl.when(pl.program_id(1) == 0)
    def _():
        _rms_norm_to(x_ref, nw_ref, xn_ref, RMS_EPS)
    xn = xn_ref[...]
    val = _dot(xn, wv_ref[...]) + bv_ref[...]
    gate = _dot(xn, wg_ref[...]) + bg_ref[...]
    o_ref[...] = (val * (1.0 / (1.0 + jnp.exp(-gate)))).astype(o_ref.dtype)


def _norm_matmul(x, nw, w, *, tm, tn, out_dtype):
    l, d = x.shape
    n = w.shape[1]
    return pl.pallas_call(
        _nmm_plain_kernel,
        out_shape=jax.ShapeDtypeStruct((l, n), out_dtype),
        grid=(l // tm, n // tn),
        in_specs=[pl.BlockSpec((tm, d), lambda i, j: (i, 0)),
                  pl.BlockSpec((1, d), lambda i, j: (0, 0)),
                  pl.BlockSpec((d, tn), lambda i, j: (0, j))],
        out_specs=pl.BlockSpec((tm, tn), lambda i, j: (i, j)),
        scratch_shapes=[pltpu.VMEM((tm, d), BF16)],
        compiler_params=_params("parallel", "arbitrary"),
        name="norm_matmul",
    )(x, _row(nw), w)


def _norm_matmul_qkv(x, nw, w, gains, *, n_norm_cols, hd, tm, tn):
    l, d = x.shape
    n = w.shape[1]
    kern = functools.partial(_nmm_qkv_kernel, n_norm_tiles=n_norm_cols // tn, hd=hd)
    return pl.pallas_call(
        kern,
        out_shape=jax.ShapeDtypeStruct((l, n), BF16),
        grid=(l // tm, n // tn),
        in_specs=[pl.BlockSpec((tm, d), lambda i, j: (i, 0)),
                  pl.BlockSpec((1, d), lambda i, j: (0, 0)),
                  pl.BlockSpec((d, tn), lambda i, j: (0, j)),
                  pl.BlockSpec((1, tn), lambda i, j: (0, j))],
        out_specs=pl.BlockSpec((tm, tn), lambda i, j: (i, j)),
        scratch_shapes=[pltpu.VMEM((tm, d), BF16)],
        compiler_params=_params("parallel", "arbitrary"),
        name="norm_matmul_qkv",
    )(x, _row(nw), w, _row(gains))


def _norm_matmul_glu(x, nw, w, b, *, tm, tn):
    l, d = x.shape
    n = w.shape[1] // 2
    nt = n // tn
    b = _row(b)
    return pl.pallas_call(
        _nmm_glu_kernel,
        out_shape=jax.ShapeDtypeStruct((l, n), F32),
        grid=(l // tm, nt),
        in_specs=[pl.BlockSpec((tm, d), lambda i, j: (i, 0)),
                  pl.BlockSpec((1, d), lambda i, j: (0, 0)),
                  pl.BlockSpec((d, tn), lambda i, j: (0, j)),
                  pl.BlockSpec((d, tn), lambda i, j: (0, j + nt)),
                  pl.BlockSpec((1, tn), lambda i, j: (0, j)),
                  pl.BlockSpec((1, tn), lambda i, j: (0, j + nt))],
        out_specs=pl.BlockSpec((tm, tn), lambda i, j: (i, j)),
        scratch_shapes=[pltpu.VMEM((tm, d), BF16)],
        compiler_params=_params("parallel", "arbitrary"),
        name="norm_matmul_glu",
    )(x, _row(nw), w, w, b, b)


def _mm_res_kernel(a_ref, w_ref, r_ref, o_ref):
    o_ref[...] = r_ref[...] + _dot(a_ref[...], w_ref[...])


def _matmul_residual(a, w, res, *, tm, tn):
    l, k = a.shape
    n = w.shape[1]
    return pl.pallas_call(
        _mm_res_kernel,
        out_shape=jax.ShapeDtypeStruct((l, n), F32),
        grid=(l // tm, n // tn),
        in_specs=[pl.BlockSpec((tm, k), lambda i, j: (i, 0)),
                  pl.BlockSpec((k, tn), lambda i, j: (0, j)),
                  pl.BlockSpec((tm, tn), lambda i, j: (i, j))],
        out_specs=pl.BlockSpec((tm, tn), lambda i, j: (i, j)),
        compiler_params=_params("parallel", "arbitrary"),
        name="matmul_residual",
    )(a, w, res)


def _mlp_kernel(x_ref, nw_ref, wu_ref, wd_ref, o_ref, xn_ref):
    j = pl.program_id(1)

    @pl.when(j == 0)
    def _():
        _rms_norm_to(x_ref, nw_ref, xn_ref, RMS_EPS)

    h = jnp.maximum(_dot(xn_ref[...], wu_ref[...]), 0.0)
    contrib = _dot((h * h).astype(BF16), wd_ref[...])

    @pl.when(j == 0)
    def _():
        o_ref[...] = x_ref[...] + contrib

    @pl.when(j > 0)
    def _():
        o_ref[...] += contrib


def _mlp(x, nw, w_up, w_down, *, tm, th):
    l, d = x.shape
    hid = w_up.shape[1]
    return pl.pallas_call(
        _mlp_kernel,
        out_shape=jax.ShapeDtypeStruct((l, d), F32),
        grid=(l // tm, hid // th),
        in_specs=[pl.BlockSpec((tm, d), lambda i, j: (i, 0)),
                  pl.BlockSpec((1, d), lambda i, j: (0, 0)),
                  pl.BlockSpec((d, th), lambda i, j: (0, j)),
                  pl.BlockSpec((th, d), lambda i, j: (j, 0))],
        out_specs=pl.BlockSpec((tm, d), lambda i, j: (i, 0)),
        scratch_shapes=[pltpu.VMEM((tm, d), BF16)],
        compiler_params=_params("parallel", "arbitrary"),
        name="mlp",
    )(x, _row(nw), w_up, w_down)


def _sb_attn_kernel(q_ref, k_ref, v_ref, u_ref, o_ref, acc_ref, c_ref, *, tq, kb):
    i = pl.program_id(1)
    nsub = tq // kb
    reps = kb // V7X_LANES
    acc_ref[...] = jnp.zeros_like(acc_ref)
    c_ref[...] = jnp.zeros_like(c_ref)

    def span(rows, start, nkb, masked):
        q = q_ref[rows, :]
        kk = k_ref[pl.ds(start, nkb * kb), :]
        vv = v_ref[pl.ds(start, nkb * kb), :]
        z = lax.dot_general(q, kk, (((1,), (1,)), ((), ())), preferred_element_type=F32)
        soft = jnp.maximum(z, 0.0) + jnp.log(1.0 + jnp.exp2(-jnp.abs(z))) * LOG2_E
        log_beta = z - soft
        if masked:
            row_id = lax.broadcasted_iota(jnp.int32, z.shape, 0)
            col_id = lax.broadcasted_iota(jnp.int32, z.shape, 1)
            strict = col_id < row_id
            soft = jnp.where(strict, soft, 0.0)
        c = c_ref[rows, :]
        atts = [None] * nkb
        for m in reversed(range(nkb)):
            cols = slice(m * kb, (m + 1) * kb)
            sm = soft[:, cols]
            r = _dot(sm.astype(BF16), u_ref[...])
            a = jnp.exp2(log_beta[:, cols] + r + jnp.concatenate([c] * reps, axis=1))
            if masked:
                a = jnp.where(strict, a, 0.0)
            atts[m] = a.astype(BF16)
            c = c - jnp.sum(sm, axis=-1, keepdims=True)
        c_ref[rows, :] = c
        att = atts[0] if nkb == 1 else jnp.concatenate(atts, axis=1)
        acc_ref[rows, :] += _dot(att, vv)

    for m in reversed(range(nsub)):
        span(slice(m * kb, tq), pl.multiple_of(i * tq + m * kb, kb), 1, True)

    def block(t, carry):
        span(slice(0, tq), pl.multiple_of((i - 1 - t) * tq, tq), nsub, False)
        return carry
    lax.fori_loop(0, i, block, 0)

    o_ref[...] = acc_ref[...].astype(o_ref.dtype)


def _sb_attention(qkv, *, heads, hd, tq):
    l = qkv.shape[0]
    kb = SB_BLOCK_KEYS
    j_id = lax.broadcasted_iota(jnp.int32, (kb, kb), 0)
    s_id = lax.broadcasted_iota(jnp.int32, (kb, kb), 1)
    u = jnp.where(j_id > s_id, -1.0, 0.0).astype(BF16)
    kern = functools.partial(_sb_attn_kernel, tq=tq, kb=kb)
    return pl.pallas_call(
        kern,
        out_shape=jax.ShapeDtypeStruct((l, heads * hd), BF16),
        grid=(heads, l // tq),
        in_specs=[pl.BlockSpec((tq, hd), lambda h, i: (i, h)),
                  pl.BlockSpec((l, hd), lambda h, i: (0, heads + h)),
                  pl.BlockSpec((l, hd), lambda h, i: (0, 2 * heads + h)),
                  pl.BlockSpec((kb, kb), lambda h, i: (0, 0))],
        out_specs=pl.BlockSpec((tq, hd), lambda h, i: (i, h)),
        scratch_shapes=[pltpu.VMEM((tq, hd), F32), pltpu.VMEM((tq, V7X_LANES), F32)],
        compiler_params=_params("parallel", "arbitrary"),
        name="sb_attention",
    )(qkv, qkv, qkv, u)


def _cf_conv_kernel(u_ref, halo_ref, dw_ref, dwb_ref, lnw_ref, lnb_ref, wo_ref, bo_ref, x_ref,
                    o_ref, ext_ref, y_ref, *, taps, halo, rows):
    i = pl.program_id(0)
    tl, ch = u_ref.shape

    @pl.when(i == 0)
    def _():
        ext_ref[0:halo, :] = jnp.zeros((halo, ch), F32)

    @pl.when(i > 0)
    def _():
        ext_ref[0:halo, :] = halo_ref[...]

    ext_ref[halo:, :] = u_ref[...]

    off = halo - (taps - 1)

    def lane_block(c, carry):
        c0 = pl.multiple_of(c * V7X_LANES, V7X_LANES)
        lanes = pl.ds(c0, V7X_LANES)
        for r0 in range(0, tl, rows):
            acc = jnp.zeros((rows, V7X_LANES), F32) + dwb_ref[:, lanes]
            for k in range(taps):
                acc = acc + dw_ref[k:k + 1, lanes] * ext_ref[r0 + off + k:r0 + off + k + rows, lanes]
            y_ref[r0:r0 + rows, lanes] = acc
        return carry
    lax.fori_loop(0, ch // V7X_LANES, lane_block, 0)

    y = y_ref[...]
    mu = jnp.mean(y, axis=-1, keepdims=True)
    yc = y - mu
    var = jnp.mean(yc * yc, axis=-1, keepdims=True)
    v = yc * lax.rsqrt(var + LN_EPS) * lnw_ref[...] + lnb_ref[...]
    v = _silu(v).astype(BF16)
    o_ref[...] = x_ref[...] + _dot(v, wo_ref[...]) + bo_ref[...]


def _conformer_conv_out(u, dw_w, dw_b, ln_w, ln_b, w_out, b_out, x, *, tl):
    l, ch = u.shape
    taps = dw_w.shape[0]
    halo = 32
    assert taps - 1 <= halo and tl % halo == 0
    kern = functools.partial(_cf_conv_kernel, taps=taps, halo=halo, rows=64)
    hb = tl // halo
    return pl.pallas_call(
        kern,
        out_shape=jax.ShapeDtypeStruct((l, ch), F32),
        grid=(l // tl,),
        in_specs=[pl.BlockSpec((tl, ch), lambda i: (i, 0)),
                  pl.BlockSpec((halo, ch), lambda i: (jnp.maximum(i * hb - 1, 0), 0)),
                  pl.BlockSpec((taps, ch), lambda i: (0, 0)),
                  pl.BlockSpec((1, ch), lambda i: (0, 0)),
                  pl.BlockSpec((1, ch), lambda i: (0, 0)),
                  pl.BlockSpec((1, ch), lambda i: (0, 0)),
                  pl.BlockSpec((ch, ch), lambda i: (0, 0)),
                  pl.BlockSpec((1, ch), lambda i: (0, 0)),
                  pl.BlockSpec((tl, ch), lambda i: (i, 0))],
        out_specs=pl.BlockSpec((tl, ch), lambda i: (i, 0)),
        scratch_shapes=[pltpu.VMEM((tl + halo, ch), F32), pltpu.VMEM((tl, ch), F32)],
        compiler_params=_params("arbitrary"),
        name="conformer_conv_out",
    )(u, u, dw_w, _row(dw_b), _row(ln_w), _row(ln_b), w_out, _row(b_out), x)


def _m2_conv_kernel(u_ref, halo_ref, w_ref, b_ref, o_ref, ext_ref, *, taps, halo):
    i = pl.program_id(0)
    tl, tc = u_ref.shape

    @pl.when(i == 0)
    def _():
        ext_ref[0:halo, :] = jnp.zeros((halo, tc), F32)

    @pl.when(i > 0)
    def _():
        ext_ref[0:halo, :] = halo_ref[...]

    ext_ref[halo:, :] = u_ref[...]
    off = halo - (taps - 1)
    acc = jnp.zeros((tl, tc), F32) + b_ref[...]
    for k in range(taps):
        acc = acc + w_ref[k:k + 1, :] * ext_ref[off + k:off + k + tl, :]
    o_ref[...] = _silu(acc)


def _m2_conv(zx, conv_w, conv_b, *, col0, tl, tc):
    l = zx.shape[0]
    taps, c = conv_w.shape
    halo = 8
    kern = functools.partial(_m2_conv_kernel, taps=taps, halo=halo)
    hb = tl // halo
    cb0 = col0 // tc
    return pl.pallas_call(
        kern,
        out_shape=jax.ShapeDtypeStruct((l, c), F32),
        grid=(l // tl, c // tc),
        in_specs=[pl.BlockSpec((tl, tc), lambda i, j: (i, cb0 + j)),
                  pl.BlockSpec((halo, tc), lambda i, j: (jnp.maximum(i * hb - 1, 0), cb0 + j)),
                  pl.BlockSpec((taps, tc), lambda i, j: (0, j)),
                  pl.BlockSpec((1, tc), lambda i, j: (0, j))],
        out_specs=pl.BlockSpec((tl, tc), lambda i, j: (i, j)),
        scratch_shapes=[pltpu.VMEM((tl + halo, tc), F32)],
        compiler_params=_params("arbitrary", "arbitrary"),
        name="m2_conv",
    )(zx, zx, conv_w, _row(conv_b))


def _ssd_kernel(z_ref, xs_ref, b_ref, c_ref, dt_ref, dtb_ref, e5_ref, e10_ref, al5_ref, al10_ref,
                d5_ref, nw_ref, tril_ref, o_ref, s_ref, *, heads_per_group, hp):
    ci = pl.program_id(1)
    q = xs_ref.shape[0]

    @pl.when(ci == 0)
    def _():
        s_ref[...] = jnp.zeros_like(s_ref)

    dt = _softplus(dt_ref[...] + dtb_ref[...])
    dt_hi, dt_lo = _split_bf16(dt)
    dt5 = _dot(dt_hi, e5_ref[...]) + _dot(dt_lo, e5_ref[...])
    dt10 = _dot(dt_hi, e10_ref[...]) + _dot(dt_lo, e10_ref[...])
    tril = tril_ref[...]
    a5 = -jnp.exp(al5_ref[...])
    a10 = -jnp.exp(al10_ref[...])
    da5_hi, da5_lo = _split_bf16(dt5 * a5)
    da10_hi, da10_lo = _split_bf16(dt10 * a10)
    acum5 = _dot(tril, da5_hi) + _dot(tril, da5_lo)
    acum10 = _dot(tril, da10_hi) + _dot(tril, da10_lo)

    xs = xs_ref[...]
    xdt = xs * dt5
    last = acum5[q - 1:q, :]
    xdec = (xdt * jnp.exp(last - acum5)).astype(BF16)
    bm = b_ref[...]
    cm = c_ref[...].astype(BF16)
    state = s_ref[...]

    y = _dot(cm, state.astype(BF16)) * jnp.exp(acum5)
    s_ref[...] = state * jnp.exp(last) + _dot(bm.T.astype(BF16), xdec)

    cb = lax.dot_general(cm, bm.astype(BF16), (((1,), (1,)), ((), ())), preferred_element_type=F32)
    row_id = lax.broadcasted_iota(jnp.int32, (q, q), 0)
    col_id = lax.broadcasted_iota(jnp.int32, (q, q), 1)
    causal = col_id <= row_id
    lane_id = lax.broadcasted_iota(jnp.int32, (q, 2 * hp), 1)
    diag = []
    for pair in range(heads_per_group // 2):
        xpair = xdt[:, pair * 2 * hp:(pair + 1) * 2 * hp]
        ypair = jnp.zeros((q, 2 * hp), F32)
        for half in range(2):
            r = 2 * pair + half
            col = acum10[:, r * q:(r + 1) * q]
            seg = col - col.T
            m = (cb * jnp.exp(jnp.where(causal, seg, -jnp.inf))).astype(BF16)
            mine = (lane_id >= half * hp) & (lane_id < (half + 1) * hp)
            ypair = ypair + _dot(m, jnp.where(mine, xpair, 0.0).astype(BF16))
        diag.append(ypair)
    y = y + jnp.concatenate(diag, axis=1) + d5_ref[...] * xs

    zg = z_ref[...]
    y = y * _silu(zg)
    ms = jnp.mean(y * y, axis=-1, keepdims=True)
    o_ref[...] = (y * lax.rsqrt(ms + M2_NORM_EPS) * nw_ref[...]).astype(o_ref.dtype)


def _ssd(zx, xbc, dt_bias, a_log, d_skip, norm_w, *, d_inner, heads):
    l = zx.shape[0]
    g = M2_GROUPS
    n = M2_STATE
    qn = M2_CHUNK
    hpg = heads // g
    hp = d_inner // heads
    gw = hpg * hp
    assert qn == V7X_LANES and n == V7X_LANES and 2 * hp == V7X_LANES and heads <= V7X_LANES
    conv_dim = xbc.shape[1]
    dt_blk = (d_inner + conv_dim) // V7X_LANES

    hid = lax.broadcasted_iota(jnp.int32, (g, V7X_LANES, gw), 1)
    gid = lax.broadcasted_iota(jnp.int32, (g, V7X_LANES, gw), 0)
    lid = lax.broadcasted_iota(jnp.int32, (g, V7X_LANES, gw), 2)
    e5 = (hid == gid * hpg + lid // hp).astype(BF16)
    hid = lax.broadcasted_iota(jnp.int32, (g, V7X_LANES, hpg * qn), 1)
    gid = lax.broadcasted_iota(jnp.int32, (g, V7X_LANES, hpg * qn), 0)
    lid = lax.broadcasted_iota(jnp.int32, (g, V7X_LANES, hpg * qn), 2)
    e10 = (hid == gid * hpg + lid // qn).astype(BF16)
    tril = (lax.broadcasted_iota(jnp.int32, (qn, qn), 1)
            <= lax.broadcasted_iota(jnp.int32, (qn, qn), 0)).astype(BF16)

    dtb = jnp.zeros((1, V7X_LANES), F32).at[0, :heads].set(dt_bias)
    al5 = jnp.repeat(a_log, hp).reshape(g, 1, gw)
    al10 = jnp.repeat(a_log, qn).reshape(g, 1, hpg * qn)
    d5 = jnp.repeat(d_skip, hp).reshape(g, 1, gw)
    nw = norm_w.reshape(g, 1, gw)

    kern = functools.partial(_ssd_kernel, heads_per_group=hpg, hp=hp)
    xs_blk0 = 0
    b_blk0 = d_inner // n
    c_blk0 = (d_inner + g * n) // n
    return pl.pallas_call(
        kern,
        out_shape=jax.ShapeDtypeStruct((l, d_inner), BF16),
        grid=(g, l // qn),
        in_specs=[pl.BlockSpec((qn, gw), lambda gi, ci: (ci, gi)),
                  pl.BlockSpec((qn, gw), lambda gi, ci: (ci, xs_blk0 + gi)),
                  pl.BlockSpec((qn, n), lambda gi, ci: (ci, b_blk0 + gi)),
                  pl.BlockSpec((qn, n), lambda gi, ci: (ci, c_blk0 + gi)),
                  pl.BlockSpec((qn, V7X_LANES), lambda gi, ci: (ci, dt_blk)),
                  pl.BlockSpec((1, V7X_LANES), lambda gi, ci: (0, 0)),
                  pl.BlockSpec((None, V7X_LANES, gw), lambda gi, ci: (gi, 0, 0)),
                  pl.BlockSpec((None, V7X_LANES, hpg * qn), lambda gi, ci: (gi, 0, 0)),
                  pl.BlockSpec((None, 1, gw), lambda gi, ci: (gi, 0, 0)),
                  pl.BlockSpec((None, 1, hpg * qn), lambda gi, ci: (gi, 0, 0)),
                  pl.BlockSpec((None, 1, gw), lambda gi, ci: (gi, 0, 0)),
                  pl.BlockSpec((None, 1, gw), lambda gi, ci: (gi, 0, 0)),
                  pl.BlockSpec((qn, qn), lambda gi, ci: (0, 0))],
        out_specs=pl.BlockSpec((qn, gw), lambda gi, ci: (ci, gi)),
        scratch_shapes=[pltpu.VMEM((n, gw), F32)],
        compiler_params=_params("parallel", "arbitrary"),
        name="ssd",
    )(zx, xbc, xbc, xbc, zx, dtb, e5, e10, al5, al10, d5, nw, tril)


def _tiles():
    return dict(tm=512, tn=1024, th=512, tq=512,tl_cf=256, tl_m2=512, tc_m2=512)


def _stick_breaking_layer(x, nw, w_qkv, q_gain, k_gain, w_o, t):
    d = x.shape[1]
    hd = q_gain.shape[0]
    heads = d // hd
    scale = LOG2_E / math.sqrt(hd)
    gains = jnp.concatenate([jnp.tile(q_gain * scale, heads), jnp.tile(k_gain, heads),
                             jnp.ones((d,), F32)])
    qkv = _norm_matmul_qkv(x, nw, w_qkv.astype(BF16), gains, n_norm_cols=2 * d, hd=hd,
                           tm=t["tm"], tn=t["tn"])
    o = _sb_attention(qkv, heads=heads, hd=hd, tq=t["tq"])
    return _matmul_residual(o, w_o.astype(BF16), x, tm=t["tm"], tn=t["tn"])


def _conformer_layer(x, nw, w_in, b_in, dw_w, dw_b, ln_w, ln_b, w_out, b_out, t):
    u = _norm_matmul_glu(x, nw, w_in.astype(BF16), b_in, tm=t["tm"], tn=t["tn"])
    return _conformer_conv_out(u, dw_w, dw_b, ln_w, ln_b, w_out.astype(BF16), b_out, x,
                               tl=t["tl_cf"])


def _mamba2_layer(x, nw, w_in, conv_w, conv_b, dt_bias, a_log, d_skip, norm_w, w_out, t):
    d_inner = norm_w.shape[0]
    heads = a_log.shape[0]
    n_in = w_in.shape[1]
    n_pad = -(-n_in // (9 * V7X_LANES)) * (9 * V7X_LANES)
    w_pad = jnp.pad(w_in.astype(BF16), ((0, 0), (0, n_pad - n_in)))
    zx = _norm_matmul(x, nw, w_pad, tm=t["tm"], tn=9 * V7X_LANES, out_dtype=F32)
    xbc = _m2_conv(zx, conv_w, conv_b, col0=d_inner, tl=t["tl_m2"], tc=t["tc_m2"])
    y = _ssd(zx, xbc, dt_bias, a_log, d_skip, norm_w, d_inner=d_inner, heads=heads)
    return _matmul_residual(y, w_out.astype(BF16), x, tm=t["tm"], tn=t["tn"])


def kernel(x, norm_mix_w, norm_mlp_w, sb_w_qkv, sb_q_norm_w, sb_k_norm_w, sb_w_o, cf_w_in, cf_b_in, cf_dw_w, cf_dw_b, cf_ln_w, cf_ln_b, cf_w_out, cf_b_out, m2_w_in, m2_conv_w, m2_conv_b, m2_dt_bias, m2_a_log, m2_d, m2_norm_w, m2_w_out, mlp_w_up, mlp_w_down):
    b, l, d = x.shape
    depth = norm_mix_w.shape[0]
    t = _tiles()
    outs = []
    for bi in range(b):
        h = x[bi]
        for i in range(depth):
            kind, j = i % N_MIXERS, i // N_MIXERS
            if kind == 0:
                h = _stick_breaking_layer(h, norm_mix_w[i], sb_w_qkv[j], sb_q_norm_w[j],
                                          sb_k_norm_w[j], sb_w_o[j], t)
            elif kind == 1:
                h = _conformer_layer(h, norm_mix_w[i], cf_w_in[j], cf_b_in[j], cf_dw_w[j],
                                     cf_dw_b[j], cf_ln_w[j], cf_ln_b[j], cf_w_out[j],
                                     cf_b_out[j], t)
            else:
                h = _mamba2_layer(h, norm_mix_w[i], m2_w_in[j], m2_conv_w[j], m2_conv_b[j],
                                  m2_dt_bias[j], m2_a_log[j], m2_d[j], m2_norm_w[j],
                                  m2_w_out[j], t)
            h = _mlp(h, norm_mlp_w[i], mlp_w_up[i].astype(BF16), mlp_w_down[i].astype(BF16),
                     tm=t["tm"], th=t["th"])
        outs.append(h)
    return jnp.stack(outs)
```

```python
import functools
import math

import jax
import jax.numpy as jnp
from jax import lax
from jax.experimental import pallas as pl
from jax.experimental.pallas import tpu as pltpu

F32 = jnp.float32
BF16 = jnp.bfloat16

N_MIXERS = 3
SB_BLOCK_KEYS = 256
LOG2_E = math.log2(math.e)
M2_GROUPS = 8
M2_STATE = 128
M2_CHUNK = 128
RMS_EPS = 1e-6
LN_EPS = 1e-5
M2_NORM_EPS = 1e-5

V7X_LANES = 128
V7X_VMEM_BYTES = 64 * 1024 * 1024
VMEM_LIMIT_BYTES = V7X_VMEM_BYTES - 8 * 1024 * 1024


def _params(*sem):
    return pltpu.CompilerParams(dimension_semantics=sem, vmem_limit_bytes=VMEM_LIMIT_BYTES)


def _row(v):
    return v.reshape(1, -1)


def _rms_norm_to(x_ref, nw_ref, xn_ref, eps, rows=64):
    def body(r, carry):
        r0 = pl.multiple_of(r * rows, rows)
        x = x_ref[pl.ds(r0, rows), :]
        ms = jnp.mean(x * x, axis=-1, keepdims=True)
        xn_ref[pl.ds(r0, rows), :] = (x * lax.rsqrt(ms + eps) * nw_ref[...]).astype(xn_ref.dtype)
        return carry
    lax.fori_loop(0, x_ref.shape[0] // rows, body, 0)


def _split_bf16(x):
    hi = x.astype(BF16)
    lo = (x - hi.astype(F32)).astype(BF16)
    return hi, lo


def _dot(a, b):
    return jnp.dot(a, b, preferred_element_type=F32)


def _silu(x):
    return x * (1.0 / (1.0 + jnp.exp(-x)))


def _softplus(x):
    return jnp.maximum(x, 0.0) + jnp.log(1.0 + jnp.exp(-jnp.abs(x)))


def _nmm_plain_kernel(x_ref, nw_ref, w_ref, o_ref, xn_ref):
    @pl.when(pl.program_id(1) == 0)
    def _():
        _rms_norm_to(x_ref, nw_ref, xn_ref, RMS_EPS)
    o_ref[...] = _dot(xn_ref[...], w_ref[...]).astype(o_ref.dtype)


def _nmm_qkv_kernel(x_ref, nw_ref, w_ref, g_ref, o_ref, xn_ref, *, n_norm_tiles, hd):
    j = pl.program_id(1)

    @pl.when(j == 0)
    def _():
        _rms_norm_to(x_ref, nw_ref, xn_ref, RMS_EPS)

    acc = _dot(xn_ref[...], w_ref[...])

    @pl.when(j < n_norm_tiles)
    def _():
        for c in range(acc.shape[1] // hd):
            cols = slice(c * hd, (c + 1) * hd)
            blk = acc[:, cols]
            ms = jnp.mean(blk * blk, axis=-1, keepdims=True)
            o_ref[:, cols] = (blk * lax.rsqrt(ms + RMS_EPS) * g_ref[:, cols]).astype(o_ref.dtype)

    @pl.when(j >= n_norm_tiles)
    def _():
        o_ref[...] = acc.astype(o_ref.dtype)


def _nmm_glu_kernel(x_ref, nw_ref, wv_ref, wg_ref, bv_ref, bg_ref, o_ref, xn_ref):
    @pl.when(pl.program_id(1) == 0)
    def _():
        _rms_norm_to(x_ref, nw_ref, xn_ref, RMS_EPS)
    xn = xn_ref[...]
    val = _dot(xn, wv_ref[...]) + bv_ref[...]
    gate = _dot(xn, wg_ref[...]) + bg_ref[...]
    o_ref[...] = (val * (1.0 / (1.0 + jnp.exp(-gate)))).astype(o_ref.dtype)


def _norm_matmul(x, nw, w, *, tm, tn, out_dtype):
    l, d = x.shape
    n = w.shape[1]
    return pl.pallas_call(
        _nmm_plain_kernel,
        out_shape=jax.ShapeDtypeStruct((l, n), out_dtype),
        grid=(l // tm, n // tn),
        in_specs=[pl.BlockSpec((tm, d), lambda i, j: (i, 0)),
                  pl.BlockSpec((1, d), lambda i, j: (0, 0)),
                  pl.BlockSpec((d, tn), lambda i, j: (0, j))],
        out_specs=pl.BlockSpec((tm, tn), lambda i, j: (i, j)),
        scratch_shapes=[pltpu.VMEM((tm, d), BF16)],
        compiler_params=_params("parallel", "arbitrary"),
        name="norm_matmul",
    )(x, _row(nw), w)


def _norm_matmul_qkv(x, nw, w, gains, *, n_norm_cols, hd, tm, tn):
    l, d = x.shape
    n = w.shape[1]
    kern = functools.partial(_nmm_qkv_kernel, n_norm_tiles=n_norm_cols // tn, hd=hd)
    return pl.pallas_call(
        kern,
        out_shape=jax.ShapeDtypeStruct((l, n), BF16),
        grid=(l // tm, n // tn),
        in_specs=[pl.BlockSpec((tm, d), lambda i, j: (i, 0)),
                  pl.BlockSpec((1, d), lambda i, j: (0, 0)),
                  pl.BlockSpec((d, tn), lambda i, j: (0, j)),
                  pl.BlockSpec((1, tn), lambda i, j: (0, j))],
        out_specs=pl.BlockSpec((tm, tn), lambda i, j: (i, j)),
        scratch_shapes=[pltpu.VMEM((tm, d), BF16)],
        compiler_params=_params("parallel", "arbitrary"),
        name="norm_matmul_qkv",
    )(x, _row(nw), w, _row(gains))


def _norm_matmul_glu(x, nw, w, b, *, tm, tn):
    l, d = x.shape
    n = w.shape[1] // 2
    nt = n // tn
    b = _row(b)
    return pl.pallas_call(
        _nmm_glu_kernel,
        out_shape=jax.ShapeDtypeStruct((l, n), F32),
        grid=(l // tm, nt),
        in_specs=[pl.BlockSpec((tm, d), lambda i, j: (i, 0)),
                  pl.BlockSpec((1, d), lambda i, j: (0, 0)),
                  pl.BlockSpec((d, tn), lambda i, j: (0, j)),
                  pl.BlockSpec((d, tn), lambda i, j: (0, j + nt)),
                  pl.BlockSpec((1, tn), lambda i, j: (0, j)),
                  pl.BlockSpec((1, tn), lambda i, j: (0, j + nt))],
        out_specs=pl.BlockSpec((tm, tn), lambda i, j: (i, j)),
        scratch_shapes=[pltpu.VMEM((tm, d), BF16)],
        compiler_params=_params("parallel", "arbitrary"),
        name="norm_matmul_glu",
    )(x, _row(nw), w, w, b, b)


def _mm_res_kernel(a_ref, w_ref, r_ref, o_ref):
    o_ref[...] = r_ref[...] + _dot(a_ref[...], w_ref[...])


def _matmul_residual(a, w, res, *, tm, tn):
    l, k = a.shape
    n = w.shape[1]
    return pl.pallas_call(
        _mm_res_kernel,
        out_shape=jax.ShapeDtypeStruct((l, n), F32),
        grid=(l // tm, n // tn),
        in_specs=[pl.BlockSpec((tm, k), lambda i, j: (i, 0)),
                  pl.BlockSpec((k, tn), lambda i, j: (0, j)),
                  pl.BlockSpec((tm, tn), lambda i, j: (i, j))],
        out_specs=pl.BlockSpec((tm, tn), lambda i, j: (i, j)),
        compiler_params=_params("parallel", "arbitrary"),
        name="matmul_residual",
    )(a, w, res)


def _mlp_kernel(x_ref, nw_ref, wu_ref, wd_ref, o_ref, xn_ref):
    j = pl.program_id(1)

    @pl.when(j == 0)
    def _():
        _rms_norm_to(x_ref, nw_ref, xn_ref, RMS_EPS)

    h = jnp.maximum(_dot(xn_ref[...], wu_ref[...]), 0.0)
    contrib = _dot((h * h).astype(BF16), wd_ref[...])

    @pl.when(j == 0)
    def _():
        o_ref[...] = x_ref[...] + contrib

    @pl.when(j > 0)
    def _():
        o_ref[...] += contrib


def _mlp(x, nw, w_up, w_down, *, tm, th):
    l, d = x.shape
    hid = w_up.shape[1]
    return pl.pallas_call(
        _mlp_kernel,
        out_shape=jax.ShapeDtypeStruct((l, d), F32),
        grid=(l // tm, hid // th),
        in_specs=[pl.BlockSpec((tm, d), lambda i, j: (i, 0)),
                  pl.BlockSpec((1, d), lambda i, j: (0, 0)),
                  pl.BlockSpec((d, th), lambda i, j: (0, j)),
                  pl.BlockSpec((th, d), lambda i, j: (j, 0))],
        out_specs=pl.BlockSpec((tm, d), lambda i, j: (i, 0)),
        scratch_shapes=[pltpu.VMEM((tm, d), BF16)],
        compiler_params=_params("parallel", "arbitrary"),
        name="mlp",
    )(x, _row(nw), w_up, w_down)


def _sb_attn_kernel(q_ref, k_ref, v_ref, u_ref, o_ref, acc_ref, c_ref, z_ref, att_ref, *, tq, kb):
    i = pl.program_id(1)
    nsub = tq // kb
    reps = kb // V7X_LANES
    acc_ref[...] = jnp.zeros_like(acc_ref)
    c_ref[...] = jnp.zeros_like(c_ref)

    def scores(rows, start, nkeys):
        kk = k_ref[pl.ds(start, nkeys), :]
        return lax.dot_general(q_ref[rows, :], kk, (((1,), (1,)), ((), ())),
                               preferred_element_type=F32)

    def weights(z, rows, masked):
        soft = jnp.maximum(z, 0.0) + jnp.log(1.0 + jnp.exp2(-jnp.abs(z))) * LOG2_E
        log_beta = z - soft
        if masked:
            row_id = lax.broadcasted_iota(jnp.int32, z.shape, 0)
            col_id = lax.broadcasted_iota(jnp.int32, z.shape, 1)
            strict = col_id < row_id
            soft = jnp.where(strict, soft, 0.0)
        c = c_ref[rows, :]
        nkb = z.shape[1] // kb
        atts = [None] * nkb
        for m in reversed(range(nkb)):
            cols = slice(m * kb, (m + 1) * kb)
            sm = soft[:, cols]
            r = _dot(sm.astype(BF16), u_ref[...])
            a = jnp.exp2(log_beta[:, cols] + r + jnp.concatenate([c] * reps, axis=1))
            if masked:
                a = jnp.where(strict, a, 0.0)
            atts[m] = a.astype(BF16)
            c = c - jnp.sum(sm, axis=-1, keepdims=True)
        c_ref[rows, :] = c
        return atts[0] if nkb == 1 else jnp.concatenate(atts, axis=1)

    for m in reversed(range(nsub)):
        rows = slice(m * kb, tq)
        start = pl.multiple_of(i * tq + m * kb, kb)
        att = weights(scores(rows, start, kb), rows, True)
        acc_ref[rows, :] += _dot(att, v_ref[pl.ds(start, kb), :])

    def key_start(t):
        return pl.multiple_of(jnp.clip(i - 1 - t, 0, i) * tq, tq)

    rows = slice(0, tq)
    odd = i % 2

    @pl.when(odd == 1)
    def _():
        att = weights(scores(rows, key_start(0), tq), rows, False)
        acc_ref[...] += _dot(att, v_ref[pl.ds(key_start(0), tq), :])

    z_ref[0] = scores(rows, key_start(odd), tq)
    att_ref[1] = jnp.zeros(att_ref.shape[1:], att_ref.dtype)

    def pair(p, carry):
        t = odd + 2 * p
        for half in range(2):
            acc_ref[...] += _dot(att_ref[1 - half], v_ref[pl.ds(key_start(t + half - 1), tq), :])
            z = z_ref[half]
            z_ref[1 - half] = scores(rows, key_start(t + half + 1), tq)
            att_ref[half] = weights(z, rows, False)
        return carry
    lax.fori_loop(0, i // 2, pair, 0)

    last = _dot(att_ref[1], v_ref[pl.ds(0, tq), :])
    o_ref[...] = (acc_ref[...] + last).astype(o_ref.dtype)


def _sb_attention(qkv, *, heads, hd, tq):
    l = qkv.shape[0]
    kb = SB_BLOCK_KEYS
    j_id = lax.broadcasted_iota(jnp.int32, (kb, kb), 0)
    s_id = lax.broadcasted_iota(jnp.int32, (kb, kb), 1)
    u = jnp.where(j_id > s_id, -1.0, 0.0).astype(BF16)
    kern = functools.partial(_sb_attn_kernel, tq=tq, kb=kb)
    return pl.pallas_call(
        kern,
        out_shape=jax.ShapeDtypeStruct((l, heads * hd), BF16),
        grid=(heads, l // tq),
        in_specs=[pl.BlockSpec((tq, hd), lambda h, i: (i, h)),
                  pl.BlockSpec((l, hd), lambda h, i: (0, heads + h)),
                  pl.BlockSpec((l, hd), lambda h, i: (0, 2 * heads + h)),
                  pl.BlockSpec((kb, kb), lambda h, i: (0, 0))],
        out_specs=pl.BlockSpec((tq, hd), lambda h, i: (i, h)),
        scratch_shapes=[pltpu.VMEM((tq, hd), F32), pltpu.VMEM((tq, V7X_LANES), F32),
                        pltpu.VMEM((2, tq, tq), F32), pltpu.VMEM((2, tq, tq), BF16)],
        compiler_params=_params("parallel", "arbitrary"),
        name="sb_attention",
    )(qkv, qkv, qkv, u)


def _cf_conv_kernel(u_ref, halo_ref, dw_ref, dwb_ref, lnw_ref, lnb_ref, wo_ref, bo_ref, x_ref,
                    o_ref, ext_ref, y_ref, *, taps, halo, rows):
    i = pl.program_id(0)
    tl, ch = u_ref.shape

    @pl.when(i == 0)
    def _():
        ext_ref[0:halo, :] = jnp.zeros((halo, ch), F32)

    @pl.when(i > 0)
    def _():
        ext_ref[0:halo, :] = halo_ref[...]

    ext_ref[halo:, :] = u_ref[...]

    off = halo - (taps - 1)

    def lane_block(c, carry):
        c0 = pl.multiple_of(c * V7X_LANES, V7X_LANES)
        lanes = pl.ds(c0, V7X_LANES)
        for r0 in range(0, tl, rows):
            acc = jnp.zeros((rows, V7X_LANES), F32) + dwb_ref[:, lanes]
            for k in range(taps):
                acc = acc + dw_ref[k:k + 1, lanes] * ext_ref[r0 + off + k:r0 + off + k + rows, lanes]
            y_ref[r0:r0 + rows, lanes] = acc
        return carry
    lax.fori_loop(0, ch // V7X_LANES, lane_block, 0)

    y = y_ref[...]
    mu = jnp.mean(y, axis=-1, keepdims=True)
    yc = y - mu
    var = jnp.mean(yc * yc, axis=-1, keepdims=True)
    v = yc * lax.rsqrt(var + LN_EPS) * lnw_ref[...] + lnb_ref[...]
    v = _silu(v).astype(BF16)
    o_ref[...] = x_ref[...] + _dot(v, wo_ref[...]) + bo_ref[...]


def _conformer_conv_out(u, dw_w, dw_b, ln_w, ln_b, w_out, b_out, x, *, tl):
    l, ch = u.shape
    taps = dw_w.shape[0]
    halo = 32
    assert taps - 1 <= halo and tl % halo == 0
    kern = functools.partial(_cf_conv_kernel, taps=taps, halo=halo, rows=64)
    hb = tl // halo
    return pl.pallas_call(
        kern,
        out_shape=jax.ShapeDtypeStruct((l, ch), F32),
        grid=(l // tl,),
        in_specs=[pl.BlockSpec((tl, ch), lambda i: (i, 0)),
                  pl.BlockSpec((halo, ch), lambda i: (jnp.maximum(i * hb - 1, 0), 0)),
                  pl.BlockSpec((taps, ch), lambda i: (0, 0)),
                  pl.BlockSpec((1, ch), lambda i: (0, 0)),
                  pl.BlockSpec((1, ch), lambda i: (0, 0)),
                  pl.BlockSpec((1, ch), lambda i: (0, 0)),
                  pl.BlockSpec((ch, ch), lambda i: (0, 0)),
                  pl.BlockSpec((1, ch), lambda i: (0, 0)),
                  pl.BlockSpec((tl, ch), lambda i: (i, 0))],
        out_specs=pl.BlockSpec((tl, ch), lambda i: (i, 0)),
        scratch_shapes=[pltpu.VMEM((tl + halo, ch), F32), pltpu.VMEM((tl, ch), F32)],
        compiler_params=_params("arbitrary"),
        name="conformer_conv_out",
    )(u, u, dw_w, _row(dw_b), _row(ln_w), _row(ln_b), w_out, _row(b_out), x)


def _ssd_kernel(zx_ref, halo_ref, cw_ref, cb_ref, dtb_ref, al_ref, e5_ref,
                d5_ref, nw_ref, tril_ref, o_ref, s_ref, ext_ref, *,
                groups, heads_per_group, hp, n, d_inner, conv_dim, taps, halo):
    ci = pl.program_id(0)
    q = zx_ref.shape[0]
    gw = heads_per_group * hp
    xbc0 = d_inner
    dt0 = d_inner + conv_dim

    @pl.when(ci == 0)
    def _():
        s_ref[...] = jnp.zeros_like(s_ref)
        ext_ref[0:halo, :] = jnp.zeros((halo, conv_dim), F32)

    @pl.when(ci > 0)
    def _():
        ext_ref[0:halo, :] = halo_ref[:, xbc0:dt0]

    ext_ref[halo:, :] = zx_ref[:, xbc0:dt0]
    off = halo - (taps - 1)

    def conv_silu(c0, width):
        cols = slice(c0, c0 + width)
        acc = cb_ref[:, cols] + cw_ref[0:1, cols] * ext_ref[off:off + q, cols]
        for k in range(1, taps):
            acc = acc + cw_ref[k:k + 1, cols] * ext_ref[off + k:off + k + q, cols]
        return _silu(acc)

    dt = _softplus(zx_ref[:, dt0:dt0 + V7X_LANES] + dtb_ref[...])
    da_hi, da_lo = _split_bf16(dt * (-jnp.exp(al_ref[...]) * LOG2_E))
    acum = _dot(tril_ref[...], da_hi) + _dot(tril_ref[...], da_lo)
    per_head = jnp.concatenate(_split_bf16(dt) + _split_bf16(acum), axis=0)

    row_id = lax.broadcasted_iota(jnp.int32, (q, q), 0)
    col_id = lax.broadcasted_iota(jnp.int32, (q, q), 1)
    causal = col_id <= row_id
    first_half = lax.broadcasted_iota(jnp.int32, (q, 2 * hp), 1) < hp

    for g in range(groups):
        wide = _dot(per_head, e5_ref[g])
        dt5 = wide[0:q] + wide[q:2 * q]
        acum5 = wide[2 * q:3 * q] + wide[3 * q:4 * q]

        xs = conv_silu(g * gw, gw)
        bm = conv_silu(d_inner + g * n, n)
        cm = conv_silu(d_inner + groups * n + g * n, n).astype(BF16)
        xdt = xs * dt5
        last = acum5[q - 1:q, :]
        xdec = (xdt * jnp.exp2(last - acum5)).astype(BF16)
        state = s_ref[g]

        y = _dot(cm, state.astype(BF16)) * jnp.exp2(acum5)
        s_ref[g] = state * jnp.exp2(last) + _dot(bm.T.astype(BF16), xdec)

        cb = lax.dot_general(cm, bm.astype(BF16), (((1,), (1,)), ((), ())),
                             preferred_element_type=F32)
        diag = []
        for pair in range(heads_per_group // 2):
            lanes = slice(pair * 2 * hp, (pair + 1) * 2 * hp)
            xpair = xdt[:, lanes]
            cpair = acum5[:, lanes]
            swapped = pltpu.roll(cpair, hp, axis=1)
            cpair_t = cpair.T
            ypair = jnp.zeros((q, 2 * hp), F32)
            for half in range(2):
                col = jnp.where(first_half, cpair, swapped) if half == 0 else \
                    jnp.where(first_half, swapped, cpair)
                seg = col - cpair_t[half * hp:half * hp + 1, :]
                m = (cb * jnp.exp2(jnp.where(causal, seg, -jnp.inf))).astype(BF16)
                mine = first_half if half == 0 else jnp.logical_not(first_half)
                ypair = ypair + _dot(m, jnp.where(mine, xpair, 0.0).astype(BF16))
            diag.append(ypair)
        y = y + jnp.concatenate(diag, axis=1) + d5_ref[g] * xs

        y = y * _silu(zx_ref[:, g * gw:(g + 1) * gw])
        ms = jnp.mean(y * y, axis=-1, keepdims=True)
        o_ref[:, g * gw:(g + 1) * gw] = (y * lax.rsqrt(ms + M2_NORM_EPS) * nw_ref[g]).astype(o_ref.dtype)


def _ssd(zx, conv_w, conv_b, dt_bias, a_log, d_skip, norm_w, *, d_inner, heads):
    l, zw = zx.shape
    g = M2_GROUPS
    n = M2_STATE
    qn = M2_CHUNK
    hpg = heads // g
    hp = d_inner // heads
    gw = hpg * hp
    taps, conv_dim = conv_w.shape
    halo = 8
    assert qn == V7X_LANES and n == V7X_LANES and 2 * hp == V7X_LANES and heads <= V7X_LANES
    assert zw == d_inner + conv_dim + V7X_LANES and taps - 1 <= halo

    hid = lax.broadcasted_iota(jnp.int32, (g, V7X_LANES, gw), 1)
    gid = lax.broadcasted_iota(jnp.int32, (g, V7X_LANES, gw), 0)
    lid = lax.broadcasted_iota(jnp.int32, (g, V7X_LANES, gw), 2)
    e5 = (hid == gid * hpg + lid // hp).astype(BF16)
    tril = (lax.broadcasted_iota(jnp.int32, (qn, qn), 1)
            <= lax.broadcasted_iota(jnp.int32, (qn, qn), 0)).astype(BF16)

    dtb = jnp.zeros((1, V7X_LANES), F32).at[0, :heads].set(dt_bias)
    al = jnp.zeros((1, V7X_LANES), F32).at[0, :heads].set(a_log)
    d5 = jnp.repeat(d_skip, hp).reshape(g, 1, gw)
    nw = norm_w.reshape(g, 1, gw)

    kern = functools.partial(_ssd_kernel, groups=g, heads_per_group=hpg, hp=hp, n=n,
                             d_inner=d_inner, conv_dim=conv_dim, taps=taps, halo=halo)
    hb = qn // halo
    const2 = lambda ci: (0, 0)
    const3 = lambda ci: (0, 0, 0)
    return pl.pallas_call(
        kern,
        out_shape=jax.ShapeDtypeStruct((l, d_inner), BF16),
        grid=(l // qn,),
        in_specs=[pl.BlockSpec((qn, zw), lambda ci: (ci, 0)),
                  pl.BlockSpec((halo, zw), lambda ci: (jnp.maximum(ci * hb - 1, 0), 0)),
                  pl.BlockSpec((taps, conv_dim), const2),
                  pl.BlockSpec((1, conv_dim), const2),
                  pl.BlockSpec((1, V7X_LANES), const2),
                  pl.BlockSpec((1, V7X_LANES), const2),
                  pl.BlockSpec((g, V7X_LANES, gw), const3),
                  pl.BlockSpec((g, 1, gw), const3),
                  pl.BlockSpec((g, 1, gw), const3),
                  pl.BlockSpec((qn, qn), const2)],
        out_specs=pl.BlockSpec((qn, d_inner), lambda ci: (ci, 0)),
        scratch_shapes=[pltpu.VMEM((g, n, gw), F32), pltpu.VMEM((qn + halo, conv_dim), F32)],
        compiler_params=_params("arbitrary"),
        name="ssd",
    )(zx, zx, conv_w, _row(conv_b), dtb, al, e5, d5, nw, tril)


def _tiles():
    return dict(tm=512, tn=1024, th=1024, tq=512, tl_cf=256)


def _stick_breaking_layer(x, nw, w_qkv, q_gain, k_gain, w_o, t):
    d = x.shape[1]
    hd = q_gain.shape[0]
    heads = d // hd
    scale = LOG2_E / math.sqrt(hd)
    gains = jnp.concatenate([jnp.tile(q_gain * scale, heads), jnp.tile(k_gain, heads),
                             jnp.ones((d,), F32)])
    qkv = _norm_matmul_qkv(x, nw, w_qkv.astype(BF16), gains, n_norm_cols=2 * d, hd=hd,
                           tm=t["tm"], tn=t["tn"])
    o = _sb_attention(qkv, heads=heads, hd=hd, tq=t["tq"])
    return _matmul_residual(o, w_o.astype(BF16), x, tm=t["tm"], tn=t["tn"])


def _conformer_layer(x, nw, w_in, b_in, dw_w, dw_b, ln_w, ln_b, w_out, b_out, t):
    u = _norm_matmul_glu(x, nw, w_in.astype(BF16), b_in, tm=t["tm"], tn=t["tn"])
    return _conformer_conv_out(u, dw_w, dw_b, ln_w, ln_b, w_out.astype(BF16), b_out, x,
                               tl=t["tl_cf"])


def _mamba2_layer(x, nw, w_in, conv_w, conv_b, dt_bias, a_log, d_skip, norm_w, w_out, t):
    d_inner = norm_w.shape[0]
    heads = a_log.shape[0]
    n_in = w_in.shape[1]
    n_pad = -(-n_in // (9 * V7X_LANES)) * (9 * V7X_LANES)
    w_pad = jnp.pad(w_in.astype(BF16), ((0, 0), (0, n_pad - n_in)))
    zx = _norm_matmul(x, nw, w_pad, tm=t["tm"], tn=9 * V7X_LANES, out_dtype=F32)
    y = _ssd(zx, conv_w, conv_b, dt_bias, a_log, d_skip, norm_w, d_inner=d_inner, heads=heads)
    return _matmul_residual(y, w_out.astype(BF16), x, tm=t["tm"], tn=t["tn"])


def kernel(x, norm_mix_w, norm_mlp_w, sb_w_qkv, sb_q_norm_w, sb_k_norm_w, sb_w_o, cf_w_in, cf_b_in, cf_dw_w, cf_dw_b, cf_ln_w, cf_ln_b, cf_w_out, cf_b_out, m2_w_in, m2_conv_w, m2_conv_b, m2_dt_bias, m2_a_log, m2_d, m2_norm_w, m2_w_out, mlp_w_up, mlp_w_down):
    b, l, d = x.shape
    depth = norm_mix_w.shape[0]
    t = _tiles()
    outs = []
    for bi in range(b):
        h = x[bi]
        for i in range(depth):
            kind, j = i % N_MIXERS, i // N_MIXERS
            if kind == 0:
                h = _stick_breaking_layer(h, norm_mix_w[i], sb_w_qkv[j], sb_q_norm_w[j],
                                          sb_k_norm_w[j], sb_w_o[j], t)
            elif kind == 1:
                h = _conformer_layer(h, norm_mix_w[i], cf_w_in[j], cf_b_in[j], cf_dw_w[j],
                                     cf_dw_b[j], cf_ln_w[j], cf_ln_b[j], cf_w_out[j],
                                     cf_b_out[j], t)
            else:
                h = _mamba2_layer(h, norm_mix_w[i], m2_w_in[j], m2_conv_w[j], m2_conv_b[j],
                                  m2_dt_bias[j], m2_a_log[j], m2_d[j], m2_norm_w[j],
                                  m2_w_out[j], t)
            h = _mlp(h, norm_mlp_w[i], mlp_w_up[i].astype(BF16), mlp_w_down[i].astype(BF16),
                     tm=t["tm"], th=t["th"])
        outs.append(h)
    return jnp.stack(outs)
```

```python
import functools
import math

import jax
import jax.numpy as jnp
from jax import lax
from jax.experimental import pallas as pl
from jax.experimental.pallas import tpu as pltpu

F32 = jnp.float32
BF16 = jnp.bfloat16

N_MIXERS = 3
SB_BLOCK_KEYS = 256
LOG2_E = math.log2(math.e)
M2_GROUPS = 8
M2_STATE = 128
M2_CHUNK = 128
RMS_EPS = 1e-6
LN_EPS = 1e-5
M2_NORM_EPS = 1e-5

V7X_LANES = 128
V7X_VMEM_BYTES = 64 * 1024 * 1024
VMEM_LIMIT_BYTES = V7X_VMEM_BYTES - 8 * 1024 * 1024


def _params(*sem):
    return pltpu.CompilerParams(dimension_semantics=sem, vmem_limit_bytes=VMEM_LIMIT_BYTES)


def _row(v):
    return v.reshape(1, -1)


def _rms_norm_to(x_ref, nw_ref, xn_ref, eps, rows=64):
    def body(r, carry):
        r0 = pl.multiple_of(r * rows, rows)
        x = x_ref[pl.ds(r0, rows), :]
        ms = jnp.mean(x * x, axis=-1, keepdims=True)
        xn_ref[pl.ds(r0, rows), :] = (x * lax.rsqrt(ms + eps) * nw_ref[...]).astype(xn_ref.dtype)
        return carry
    lax.fori_loop(0, x_ref.shape[0] // rows, body, 0)


def _split_bf16(x):
    hi = x.astype(BF16)
    lo = (x - hi.astype(F32)).astype(BF16)
    return hi, lo


def _dot(a, b):
    return jnp.dot(a, b, preferred_element_type=F32)


def _silu(x):
    return x * (1.0 / (1.0 + jnp.exp(-x)))


def _softplus(x):
    return jnp.maximum(x, 0.0) + jnp.log(1.0 + jnp.exp(-jnp.abs(x)))


def _nmm_plain_kernel(x_ref, nw_ref, w_ref, o_ref, xn_ref):
    @pl.when(pl.program_id(1) == 0)
    def _():
        _rms_norm_to(x_ref, nw_ref, xn_ref, RMS_EPS)
    o_ref[...] = _dot(xn_ref[...], w_ref[...]).astype(o_ref.dtype)


def _nmm_qkv_kernel(x_ref, nw_ref, w_ref, g_ref, o_ref, xn_ref, *, n_norm_tiles, hd):
    j = pl.program_id(1)

    @pl.when(j == 0)
    def _():
        _rms_norm_to(x_ref, nw_ref, xn_ref, RMS_EPS)

    acc = _dot(xn_ref[...], w_ref[...])

    @pl.when(j < n_norm_tiles)
    def _():
        for c in range(acc.shape[1] // hd):
            cols = slice(c * hd, (c + 1) * hd)
            blk = acc[:, cols]
            ms = jnp.mean(blk * blk, axis=-1, keepdims=True)
            o_ref[:, cols] = (blk * lax.rsqrt(ms + RMS_EPS) * g_ref[:, cols]).astype(o_ref.dtype)

    @pl.when(j >= n_norm_tiles)
    def _():
        o_ref[...] = acc.astype(o_ref.dtype)


def _nmm_glu_kernel(x_ref, nw_ref, wv_ref, wg_ref, bv_ref, bg_ref, o_ref, xn_ref):
    @pl.when(pl.program_id(1) == 0)
    def _():
        _rms_norm_to(x_ref, nw_ref, xn_ref, RMS_EPS)
    xn = xn_ref[...]
    val = _dot(xn, wv_ref[...]) + bv_ref[...]
    gate = _dot(xn, wg_ref[...]) + bg_ref[...]
    o_ref[...] = (val * (1.0 / (1.0 + jnp.exp(-gate)))).astype(o_ref.dtype)


def _norm_matmul(x, nw, w, *, tm, tn, out_dtype):
    l, d = x.shape
    n = w.shape[1]
    return pl.pallas_call(
        _nmm_plain_kernel,
        out_shape=jax.ShapeDtypeStruct((l, n), out_dtype),
        grid=(l // tm, n // tn),
        in_specs=[pl.BlockSpec((tm, d), lambda i, j: (i, 0)),
                  pl.BlockSpec((1, d), lambda i, j: (0, 0)),
                  pl.BlockSpec((d, tn), lambda i, j: (0, j))],
        out_specs=pl.BlockSpec((tm, tn), lambda i, j: (i, j)),
        scratch_shapes=[pltpu.VMEM((tm, d), BF16)],
        compiler_params=_params("parallel", "arbitrary"),
        name="norm_matmul",
    )(x, _row(nw), w)


def _norm_matmul_qkv(x, nw, w, gains, *, n_norm_cols, hd, tm, tn):
    l, d = x.shape
    n = w.shape[1]
    kern = functools.partial(_nmm_qkv_kernel, n_norm_tiles=n_norm_cols // tn, hd=hd)
    return pl.pallas_call(
        kern,
        out_shape=jax.ShapeDtypeStruct((l, n), BF16),
        grid=(l // tm, n // tn),
        in_specs=[pl.BlockSpec((tm, d), lambda i, j: (i, 0)),
                  pl.BlockSpec((1, d), lambda i, j: (0, 0)),
                  pl.BlockSpec((d, tn), lambda i, j: (0, j)),
                  pl.BlockSpec((1, tn), lambda i, j: (0, j))],
        out_specs=pl.BlockSpec((tm, tn), lambda i, j: (i, j)),
        scratch_shapes=[pltpu.VMEM((tm, d), BF16)],
        compiler_params=_params("parallel", "arbitrary"),
        name="norm_matmul_qkv",
    )(x, _row(nw), w, _row(gains))


def _norm_matmul_glu(x, nw, w, b, *, tm, tn):
    l, d = x.shape
    n = w.shape[1] // 2
    nt = n // tn
    b = _row(b)
    return pl.pallas_call(
        _nmm_glu_kernel,
        out_shape=jax.ShapeDtypeStruct((l, n), F32),
        grid=(l // tm, nt),
        in_specs=[pl.BlockSpec((tm, d), lambda i, j: (i, 0)),
                  pl.BlockSpec((1, d), lambda i, j: (0, 0)),
                  pl.BlockSpec((d, tn), lambda i, j: (0, j)),
                  pl.BlockSpec((d, tn), lambda i, j: (0, j + nt)),
                  pl.BlockSpec((1, tn), lambda i, j: (0, j)),
                  pl.BlockSpec((1, tn), lambda i, j: (0, j + nt))],
        out_specs=pl.BlockSpec((tm, tn), lambda i, j: (i, j)),
        scratch_shapes=[pltpu.VMEM((tm, d), BF16)],
        compiler_params=_params("parallel", "arbitrary"),
        name="norm_matmul_glu",
    )(x, _row(nw), w, w, b, b)


def _mm_res_kernel(a_ref, w_ref, r_ref, o_ref):
    o_ref[...] = r_ref[...] + _dot(a_ref[...], w_ref[...])


def _matmul_residual(a, w, res, *, tm, tn):
    l, k = a.shape
    n = w.shape[1]
    return pl.pallas_call(
        _mm_res_kernel,
        out_shape=jax.ShapeDtypeStruct((l, n), F32),
        grid=(l // tm, n // tn),
        in_specs=[pl.BlockSpec((tm, k), lambda i, j: (i, 0)),
                  pl.BlockSpec((k, tn), lambda i, j: (0, j)),
                  pl.BlockSpec((tm, tn), lambda i, j: (i, j))],
        out_specs=pl.BlockSpec((tm, tn), lambda i, j: (i, j)),
        compiler_params=_params("parallel", "arbitrary"),
        name="matmul_residual",
    )(a, w, res)


def _mlp_kernel(x_ref, nw_ref, wu_ref, wd_ref, o_ref, xn_ref):
    j = pl.program_id(1)

    @pl.when(j == 0)
    def _():
        _rms_norm_to(x_ref, nw_ref, xn_ref, RMS_EPS)

    h = jnp.maximum(_dot(xn_ref[...], wu_ref[...]), 0.0)
    contrib = _dot((h * h).astype(BF16), wd_ref[...])

    @pl.when(j == 0)
    def _():
        o_ref[...] = x_ref[...] + contrib

    @pl.when(j > 0)
    def _():
        o_ref[...] += contrib


def _mlp(x, nw, w_up, w_down, *, tm, th):
    l, d = x.shape
    hid = w_up.shape[1]
    return pl.pallas_call(
        _mlp_kernel,
        out_shape=jax.ShapeDtypeStruct((l, d), F32),
        grid=(l // tm, hid // th),
        in_specs=[pl.BlockSpec((tm, d), lambda i, j: (i, 0)),
                  pl.BlockSpec((1, d), lambda i, j: (0, 0)),
                  pl.BlockSpec((d, th), lambda i, j: (0, j)),
                  pl.BlockSpec((th, d), lambda i, j: (j, 0))],
        out_specs=pl.BlockSpec((tm, d), lambda i, j: (i, 0)),
        scratch_shapes=[pltpu.VMEM((tm, d), BF16)],
        compiler_params=_params("parallel", "arbitrary"),
        name="mlp",
    )(x, _row(nw), w_up, w_down)


def _sb_attn_kernel(q_ref, k_ref, v_ref, u_ref, o_ref, acc_ref, c_ref, z_ref, att_ref, *, tq, kb):
    i = pl.program_id(1)
    nsub = tq // kb
    reps = kb // V7X_LANES
    acc_ref[...] = jnp.zeros_like(acc_ref)
    c_ref[...] = jnp.zeros_like(c_ref)

    def scores(rows, start, nkeys):
        kk = k_ref[pl.ds(start, nkeys), :]
        return lax.dot_general(q_ref[rows, :], kk, (((1,), (1,)), ((), ())),
                               preferred_element_type=F32)

    def weights(z, rows, masked):
        soft = jnp.maximum(z, 0.0) + jnp.log(1.0 + jnp.exp2(-jnp.abs(z))) * LOG2_E
        log_beta = z - soft
        if masked:
            row_id = lax.broadcasted_iota(jnp.int32, z.shape, 0)
            col_id = lax.broadcasted_iota(jnp.int32, z.shape, 1)
            strict = col_id < row_id
            soft = jnp.where(strict, soft, 0.0)
        c = c_ref[rows, :]
        nkb = z.shape[1] // kb
        atts = [None] * nkb
        for m in reversed(range(nkb)):
            cols = slice(m * kb, (m + 1) * kb)
            sm = soft[:, cols]
            r = _dot(sm.astype(BF16), u_ref[...])
            a = jnp.exp2(log_beta[:, cols] + r + jnp.concatenate([c] * reps, axis=1))
            if masked:
                a = jnp.where(strict, a, 0.0)
            atts[m] = a.astype(BF16)
            c = c - jnp.sum(sm, axis=-1, keepdims=True)
        c_ref[rows, :] = c
        return atts[0] if nkb == 1 else jnp.concatenate(atts, axis=1)

    for m in reversed(range(nsub)):
        rows = slice(m * kb, tq)
        start = pl.multiple_of(i * tq + m * kb, kb)
        att = weights(scores(rows, start, kb), rows, True)
        acc_ref[rows, :] += _dot(att, v_ref[pl.ds(start, kb), :])

    def key_start(t):
        return pl.multiple_of(jnp.clip(i - 1 - t, 0, i) * tq, tq)

    rows = slice(0, tq)
    odd = i % 2

    @pl.when(odd == 1)
    def _():
        att = weights(scores(rows, key_start(0), tq), rows, False)
        acc_ref[...] += _dot(att, v_ref[pl.ds(key_start(0), tq), :])

    z_ref[0] = scores(rows, key_start(odd), tq)
    att_ref[1] = jnp.zeros(att_ref.shape[1:], att_ref.dtype)

    def pair(p, carry):
        t = odd + 2 * p
        for half in range(2):
            acc_ref[...] += _dot(att_ref[1 - half], v_ref[pl.ds(key_start(t + half - 1), tq), :])
            z = z_ref[half]
            z_ref[1 - half] = scores(rows, key_start(t + half + 1), tq)
            att_ref[half] = weights(z, rows, False)
        return carry
    lax.fori_loop(0, i // 2, pair, 0)

    last = _dot(att_ref[1], v_ref[pl.ds(0, tq), :])
    o_ref[...] = (acc_ref[...] + last).astype(o_ref.dtype)


def _sb_attention(qkv, *, heads, hd, tq):
    l = qkv.shape[0]
    kb = SB_BLOCK_KEYS
    j_id = lax.broadcasted_iota(jnp.int32, (kb, kb), 0)
    s_id = lax.broadcasted_iota(jnp.int32, (kb, kb), 1)
    u = jnp.where(j_id > s_id, -1.0, 0.0).astype(BF16)
    kern = functools.partial(_sb_attn_kernel, tq=tq, kb=kb)
    return pl.pallas_call(
        kern,
        out_shape=jax.ShapeDtypeStruct((l, heads * hd), BF16),
        grid=(heads, l // tq),
        in_specs=[pl.BlockSpec((tq, hd), lambda h, i: (i, h)),
                  pl.BlockSpec((l, hd), lambda h, i: (0, heads + h)),
                  pl.BlockSpec((l, hd), lambda h, i: (0, 2 * heads + h)),
                  pl.BlockSpec((kb, kb), lambda h, i: (0, 0))],
        out_specs=pl.BlockSpec((tq, hd), lambda h, i: (i, h)),
        scratch_shapes=[pltpu.VMEM((tq, hd), F32), pltpu.VMEM((tq, V7X_LANES), F32),
                        pltpu.VMEM((2, tq, tq), F32), pltpu.VMEM((2, tq, tq), BF16)],
        compiler_params=_params("parallel", "arbitrary"),
        name="sb_attention",
    )(qkv, qkv, qkv, u)


def _cf_conv_kernel(u_ref, halo_ref, dw_ref, dwb_ref, lnw_ref, lnb_ref, wo_ref, bo_ref, x_ref,
                    o_ref, ext_ref, y_ref, *, taps, halo, rows):
    i = pl.program_id(0)
    tl, ch = u_ref.shape

    @pl.when(i == 0)
    def _():
        ext_ref[0:halo, :] = jnp.zeros((halo, ch), F32)

    @pl.when(i > 0)
    def _():
        ext_ref[0:halo, :] = halo_ref[...]

    ext_ref[halo:, :] = u_ref[...]

    off = halo - (taps - 1)

    def lane_block(c, carry):
        c0 = pl.multiple_of(c * V7X_LANES, V7X_LANES)
        lanes = pl.ds(c0, V7X_LANES)
        for r0 in range(0, tl, rows):
            acc = jnp.zeros((rows, V7X_LANES), F32) + dwb_ref[:, lanes]
            for k in range(taps):
                acc = acc + dw_ref[k:k + 1, lanes] * ext_ref[r0 + off + k:r0 + off + k + rows, lanes]
            y_ref[r0:r0 + rows, lanes] = acc
        return carry
    lax.fori_loop(0, ch // V7X_LANES, lane_block, 0)

    y = y_ref[...]
    mu = jnp.mean(y, axis=-1, keepdims=True)
    yc = y - mu
    var = jnp.mean(yc * yc, axis=-1, keepdims=True)
    v = yc * lax.rsqrt(var + LN_EPS) * lnw_ref[...] + lnb_ref[...]
    v = _silu(v).astype(BF16)
    o_ref[...] = x_ref[...] + _dot(v, wo_ref[...]) + bo_ref[...]


def _conformer_conv_out(u, dw_w, dw_b, ln_w, ln_b, w_out, b_out, x, *, tl):
    l, ch = u.shape
    taps = dw_w.shape[0]
    halo = 32
    assert taps - 1 <= halo and tl % halo == 0
    kern = functools.partial(_cf_conv_kernel, taps=taps, halo=halo, rows=64)
    hb = tl // halo
    return pl.pallas_call(
        kern,
        out_shape=jax.ShapeDtypeStruct((l, ch), F32),
        grid=(l // tl,),
        in_specs=[pl.BlockSpec((tl, ch), lambda i: (i, 0)),
                  pl.BlockSpec((halo, ch), lambda i: (jnp.maximum(i * hb - 1, 0), 0)),
                  pl.BlockSpec((taps, ch), lambda i: (0, 0)),
                  pl.BlockSpec((1, ch), lambda i: (0, 0)),
                  pl.BlockSpec((1, ch), lambda i: (0, 0)),
                  pl.BlockSpec((1, ch), lambda i: (0, 0)),
                  pl.BlockSpec((ch, ch), lambda i: (0, 0)),
                  pl.BlockSpec((1, ch), lambda i: (0, 0)),
                  pl.BlockSpec((tl, ch), lambda i: (i, 0))],
        out_specs=pl.BlockSpec((tl, ch), lambda i: (i, 0)),
        scratch_shapes=[pltpu.VMEM((tl + halo, ch), F32), pltpu.VMEM((tl, ch), F32)],
        compiler_params=_params("arbitrary"),
        name="conformer_conv_out",
    )(u, u, dw_w, _row(dw_b), _row(ln_w), _row(ln_b), w_out, _row(b_out), x)


def _ssd_kernel(zx_ref, halo_ref, cw_ref, cb_ref, dtb_ref, al_ref, e5_ref,
                d5_ref, nw_ref, tril_ref, o_ref, s_ref, ext_ref, *,
                groups, heads_per_group, hp, n, d_inner, conv_dim, taps, halo):
    ci = pl.program_id(0)
    q = zx_ref.shape[0]
    gw = heads_per_group * hp
    xbc0 = d_inner
    dt0 = d_inner + conv_dim

    @pl.when(ci == 0)
    def _():
        s_ref[...] = jnp.zeros_like(s_ref)
        ext_ref[0:halo, :] = jnp.zeros((halo, conv_dim), F32)

    @pl.when(ci > 0)
    def _():
        ext_ref[0:halo, :] = halo_ref[:, xbc0:dt0]

    ext_ref[halo:, :] = zx_ref[:, xbc0:dt0]
    off = halo - (taps - 1)

    def conv_silu(c0, width):
        cols = slice(c0, c0 + width)
        acc = cb_ref[:, cols] + cw_ref[0:1, cols] * ext_ref[off:off + q, cols]
        for k in range(1, taps):
            acc = acc + cw_ref[k:k + 1, cols] * ext_ref[off + k:off + k + q, cols]
        return _silu(acc)

    dt = _softplus(zx_ref[:, dt0:dt0 + V7X_LANES] + dtb_ref[...])
    da_hi, da_lo = _split_bf16(dt * (-jnp.exp(al_ref[...]) * LOG2_E))
    acum = _dot(tril_ref[...], da_hi) + _dot(tril_ref[...], da_lo)
    per_head = jnp.concatenate(_split_bf16(dt) + _split_bf16(acum), axis=0)

    row_id = lax.broadcasted_iota(jnp.int32, (q, q), 0)
    col_id = lax.broadcasted_iota(jnp.int32, (q, q), 1)
    causal = col_id <= row_id
    first_half = lax.broadcasted_iota(jnp.int32, (q, 2 * hp), 1) < hp

    for g in range(groups):
        wide = _dot(per_head, e5_ref[g])
        dt5 = wide[0:q] + wide[q:2 * q]
        acum5 = wide[2 * q:3 * q] + wide[3 * q:4 * q]

        xs = conv_silu(g * gw, gw)
        bm = conv_silu(d_inner + g * n, n)
        cm = conv_silu(d_inner + groups * n + g * n, n).astype(BF16)
        xdt = xs * dt5
        last = acum5[q - 1:q, :]
        xdec = (xdt * jnp.exp2(last - acum5)).astype(BF16)
        state = s_ref[g]

        y = _dot(cm, state.astype(BF16)) * jnp.exp2(acum5)
        s_ref[g] = state * jnp.exp2(last) + _dot(bm.T.astype(BF16), xdec)

        cb = lax.dot_general(cm, bm.astype(BF16), (((1,), (1,)), ((), ())),
                             preferred_element_type=F32)
        diag = []
        for pair in range(heads_per_group // 2):
            lanes = slice(pair * 2 * hp, (pair + 1) * 2 * hp)
            xpair = xdt[:, lanes]
            cpair = acum5[:, lanes]
            swapped = pltpu.roll(cpair, hp, axis=1)
            cpair_t = cpair.T
            ypair = jnp.zeros((q, 2 * hp), F32)
            for half in range(2):
                col = jnp.where(first_half, cpair, swapped) if half == 0 else \
                    jnp.where(first_half, swapped, cpair)
                seg = col - cpair_t[half * hp:half * hp + 1, :]
                m = (cb * jnp.exp2(jnp.where(causal, seg, -jnp.inf))).astype(BF16)
                mine = first_half if half == 0 else jnp.logical_not(first_half)
                ypair = ypair + _dot(m, jnp.where(mine, xpair, 0.0).astype(BF16))
            diag.append(ypair)
        y = y + jnp.concatenate(diag, axis=1) + d5_ref[g] * xs

        y = y * _silu(zx_ref[:, g * gw:(g + 1) * gw])
        ms = jnp.mean(y * y, axis=-1, keepdims=True)
        o_ref[:, g * gw:(g + 1) * gw] = (y * lax.rsqrt(ms + M2_NORM_EPS) * nw_ref[g]).astype(o_ref.dtype)


def _ssd(zx, conv_w, conv_b, dt_bias, a_log, d_skip, norm_w, *, d_inner, heads):
    l, zw = zx.shape
    g = M2_GROUPS
    n = M2_STATE
    qn = M2_CHUNK
    hpg = heads // g
    hp = d_inner // heads
    gw = hpg * hp
    taps, conv_dim = conv_w.shape
    halo = 8
    assert qn == V7X_LANES and n == V7X_LANES and 2 * hp == V7X_LANES and heads <= V7X_LANES
    assert zw == d_inner + conv_dim + V7X_LANES and taps - 1 <= halo

    hid = lax.broadcasted_iota(jnp.int32, (g, V7X_LANES, gw), 1)
    gid = lax.broadcasted_iota(jnp.int32, (g, V7X_LANES, gw), 0)
    lid = lax.broadcasted_iota(jnp.int32, (g, V7X_LANES, gw), 2)
    e5 = (hid == gid * hpg + lid // hp).astype(BF16)
    tril = (lax.broadcasted_iota(jnp.int32, (qn, qn), 1)
            <= lax.broadcasted_iota(jnp.int32, (qn, qn), 0)).astype(BF16)

    dtb = jnp.zeros((1, V7X_LANES), F32).at[0, :heads].set(dt_bias)
    al = jnp.zeros((1, V7X_LANES), F32).at[0, :heads].set(a_log)
    d5 = jnp.repeat(d_skip, hp).reshape(g, 1, gw)
    nw = norm_w.reshape(g, 1, gw)

    kern = functools.partial(_ssd_kernel, groups=g, heads_per_group=hpg, hp=hp, n=n,
                             d_inner=d_inner, conv_dim=conv_dim, taps=taps, halo=halo)
    hb = qn // halo
    const2 = lambda ci: (0, 0)
    const3 = lambda ci: (0, 0, 0)
    return pl.pallas_call(
        kern,
        out_shape=jax.ShapeDtypeStruct((l, d_inner), BF16),
        grid=(l // qn,),
        in_specs=[pl.BlockSpec((qn, zw), lambda ci: (ci, 0)),
                  pl.BlockSpec((halo, zw), lambda ci: (jnp.maximum(ci * hb - 1, 0), 0)),
                  pl.BlockSpec((taps, conv_dim), const2),
                  pl.BlockSpec((1, conv_dim), const2),
                  pl.BlockSpec((1, V7X_LANES), const2),
                  pl.BlockSpec((1, V7X_LANES), const2),
                  pl.BlockSpec((g, V7X_LANES, gw), const3),
                  pl.BlockSpec((g, 1, gw), const3),
                  pl.BlockSpec((g, 1, gw), const3),
                  pl.BlockSpec((qn, qn), const2)],
        out_specs=pl.BlockSpec((qn, d_inner), lambda ci: (ci, 0)),
        scratch_shapes=[pltpu.VMEM((g, n, gw), F32), pltpu.VMEM((qn + halo, conv_dim), F32)],
        compiler_params=_params("arbitrary"),
        name="ssd",
    )(zx, zx, conv_w, _row(conv_b), dtb, al, e5, d5, nw, tril)


def _tiles():
    return dict(tm=512, tn=1024, tn_wide=2048, th=1024, tq=512, tl_cf=256)


def _stick_breaking_layer(x, nw, w_qkv, q_gain, k_gain, w_o, t):
    d = x.shape[1]
    hd = q_gain.shape[0]
    heads = d // hd
    scale = LOG2_E / math.sqrt(hd)
    gains = jnp.concatenate([jnp.tile(q_gain * scale, heads), jnp.tile(k_gain, heads),
                             jnp.ones((d,), F32)])
    qkv = _norm_matmul_qkv(x, nw, w_qkv, gains, n_norm_cols=2 * d, hd=hd,
                           tm=t["tm"], tn=t["tn_wide"])
    o = _sb_attention(qkv, heads=heads, hd=hd, tq=t["tq"])
    return _matmul_residual(o, w_o, x, tm=t["tm"], tn=t["tn_wide"])


def _conformer_layer(x, nw, w_in, b_in, dw_w, dw_b, ln_w, ln_b, w_out, b_out, t):
    u = _norm_matmul_glu(x, nw, w_in, b_in, tm=t["tm"], tn=t["tn"])
    return _conformer_conv_out(u, dw_w, dw_b, ln_w, ln_b, w_out, b_out, x, tl=t["tl_cf"])


def _mamba2_layer(x, nw, w_in, conv_w, conv_b, dt_bias, a_log, d_skip, norm_w, w_out, t):
    d_inner = norm_w.shape[0]
    heads = a_log.shape[0]
    n_in = w_in.shape[1]
    n_pad = -(-n_in // (9 * V7X_LANES)) * (9 * V7X_LANES)
    w_pad = jnp.pad(w_in, ((0, 0), (0, n_pad - n_in)))
    zx = _norm_matmul(x, nw, w_pad, tm=t["tm"], tn=9 * V7X_LANES, out_dtype=F32)
    y = _ssd(zx, conv_w, conv_b, dt_bias, a_log, d_skip, norm_w, d_inner=d_inner, heads=heads)
    return _matmul_residual(y, w_out, x, tm=t["tm"], tn=t["tn"])


def kernel(x, norm_mix_w, norm_mlp_w, sb_w_qkv, sb_q_norm_w, sb_k_norm_w, sb_w_o, cf_w_in, cf_b_in, cf_dw_w, cf_dw_b, cf_ln_w, cf_ln_b, cf_w_out, cf_b_out, m2_w_in, m2_conv_w, m2_conv_b, m2_dt_bias, m2_a_log, m2_d, m2_norm_w, m2_w_out, mlp_w_up, mlp_w_down):
    b, l, d = x.shape
    depth = norm_mix_w.shape[0]
    t = _tiles()
    (sb_w_qkv, sb_w_o, cf_w_in, cf_w_out, m2_w_in, m2_w_out, mlp_w_up, mlp_w_down) = (
        w.astype(BF16) for w in (sb_w_qkv, sb_w_o, cf_w_in, cf_w_out, m2_w_in, m2_w_out,
                                 mlp_w_up, mlp_w_down))
    outs = []
    for bi in range(b):
        h = x[bi]
        for i in range(depth):
            kind, j = i % N_MIXERS, i // N_MIXERS
            if kind == 0:
                h = _stick_breaking_layer(h, norm_mix_w[i], sb_w_qkv[j], sb_q_norm_w[j],
                                          sb_k_norm_w[j], sb_w_o[j], t)
            elif kind == 1:
                h = _conformer_layer(h, norm_mix_w[i], cf_w_in[j], cf_b_in[j], cf_dw_w[j],
                                     cf_dw_b[j], cf_ln_w[j], cf_ln_b[j], cf_w_out[j],
                                     cf_b_out[j], t)
            else:
                h = _mamba2_layer(h, norm_mix_w[i], m2_w_in[j], m2_conv_w[j], m2_conv_b[j],
                                  m2_dt_bias[j], m2_a_log[j], m2_d[j], m2_norm_w[j],
                                  m2_w_out[j], t)
            h = _mlp(h, norm_mlp_w[i], mlp_w_up[i], mlp_w_down[i], tm=t["tm"], th=t["th"])
        outs.append(h)
    return jnp.stack(outs)
```

```python
import functools
import math

import jax
import jax.numpy as jnp
from jax import lax
from jax.experimental import pallas as pl
from jax.experimental.pallas import tpu as pltpu

F32 = jnp.float32
BF16 = jnp.bfloat16

N_MIXERS = 3
SB_BLOCK_KEYS = 256
LOG2_E = math.log2(math.e)
M2_GROUPS = 8
M2_STATE = 128
M2_CHUNK = 128
RMS_EPS = 1e-6
LN_EPS = 1e-5
M2_NORM_EPS = 1e-5

V7X_LANES = 128
V7X_VMEM_BYTES = 64 * 1024 * 1024
VMEM_LIMIT_BYTES = V7X_VMEM_BYTES - 8 * 1024 * 1024


def _params(*sem):
    return pltpu.CompilerParams(dimension_semantics=sem, vmem_limit_bytes=VMEM_LIMIT_BYTES)


def _row(v):
    return v.reshape(1, -1)


def _rms_norm_to(x_ref, nw_ref, xn_ref, eps, rows=64):
    def body(r, carry):
        r0 = pl.multiple_of(r * rows, rows)
        x = x_ref[pl.ds(r0, rows), :]
        ms = jnp.mean(x * x, axis=-1, keepdims=True)
        xn_ref[pl.ds(r0, rows), :] = (x * lax.rsqrt(ms + eps) * nw_ref[...]).astype(xn_ref.dtype)
        return carry
    lax.fori_loop(0, x_ref.shape[0] // rows, body, 0)


def _split_bf16(x):
    hi = x.astype(BF16)
    lo = (x - hi.astype(F32)).astype(BF16)
    return hi, lo


def _dot(a, b):
    return jnp.dot(a, b, preferred_element_type=F32)


def _silu(x):
    return x * (1.0 / (1.0 + jnp.exp(-x)))


def _softplus(x):
    return jnp.maximum(x, 0.0) + jnp.log(1.0 + jnp.exp(-jnp.abs(x)))


def _nmm_plain_kernel(x_ref, nw_ref, w_ref, o_ref, xn_ref):
    @pl.when(pl.program_id(1) == 0)
    def _():
        _rms_norm_to(x_ref, nw_ref, xn_ref, RMS_EPS)
    o_ref[...] = _dot(xn_ref[...], w_ref[...]).astype(o_ref.dtype)


def _nmm_qkv_kernel(x_ref, nw_ref, w_ref, g_ref, o_ref, xn_ref, *, n_norm_tiles, hd):
    j = pl.program_id(1)

    @pl.when(j == 0)
    def _():
        _rms_norm_to(x_ref, nw_ref, xn_ref, RMS_EPS)

    acc = _dot(xn_ref[...], w_ref[...])

    @pl.when(j < n_norm_tiles)
    def _():
        for c in range(acc.shape[1] // hd):
            cols = slice(c * hd, (c + 1) * hd)
            blk = acc[:, cols]
            ms = jnp.mean(blk * blk, axis=-1, keepdims=True)
            o_ref[:, cols] = (blk * lax.rsqrt(ms + RMS_EPS) * g_ref[:, cols]).astype(o_ref.dtype)

    @pl.when(j >= n_norm_tiles)
    def _():
        o_ref[...] = acc.astype(o_ref.dtype)


def _nmm_glu_kernel(x_ref, nw_ref, wv_ref, wg_ref, bv_ref, bg_ref, o_ref, xn_ref):
    @pl.when(pl.program_id(1) == 0)
    def _():
        _rms_norm_to(x_ref, nw_ref, xn_ref, RMS_EPS)
    xn = xn_ref[...]
    val = _dot(xn, wv_ref[...]) + bv_ref[...]
    gate = _dot(xn, wg_ref[...]) + bg_ref[...]
    o_ref[...] = (val * (1.0 / (1.0 + jnp.exp(-gate)))).astype(o_ref.dtype)


def _norm_matmul(x, nw, w, *, tm, tn, out_dtype):
    l, d = x.shape
    n = w.shape[1]
    return pl.pallas_call(
        _nmm_plain_kernel,
        out_shape=jax.ShapeDtypeStruct((l, n), out_dtype),
        grid=(l // tm, n // tn),
        in_specs=[pl.BlockSpec((tm, d), lambda i, j: (i, 0)),
                  pl.BlockSpec((1, d), lambda i, j: (0, 0)),
                  pl.BlockSpec((d, tn), lambda i, j: (0, j))],
        out_specs=pl.BlockSpec((tm, tn), lambda i, j: (i, j)),
        scratch_shapes=[pltpu.VMEM((tm, d), BF16)],
        compiler_params=_params("parallel", "arbitrary"),
        name="norm_matmul",
    )(x, _row(nw), w)


def _norm_matmul_qkv(x, nw, w, gains, *, n_norm_cols, hd, tm, tn):
    l, d = x.shape
    n = w.shape[1]
    kern = functools.partial(_nmm_qkv_kernel, n_norm_tiles=n_norm_cols // tn, hd=hd)
    return pl.pallas_call(
        kern,
        out_shape=jax.ShapeDtypeStruct((l, n), BF16),
        grid=(l // tm, n // tn),
        in_specs=[pl.BlockSpec((tm, d), lambda i, j: (i, 0)),
                  pl.BlockSpec((1, d), lambda i, j: (0, 0)),
                  pl.BlockSpec((d, tn), lambda i, j: (0, j)),
                  pl.BlockSpec((1, tn), lambda i, j: (0, j))],
        out_specs=pl.BlockSpec((tm, tn), lambda i, j: (i, j)),
        scratch_shapes=[pltpu.VMEM((tm, d), BF16)],
        compiler_params=_params("parallel", "arbitrary"),
        name="norm_matmul_qkv",
    )(x, _row(nw), w, _row(gains))


def _norm_matmul_glu(x, nw, w, b, *, tm, tn):
    l, d = x.shape
    n = w.shape[1] // 2
    nt = n // tn
    b = _row(b)
    return pl.pallas_call(
        _nmm_glu_kernel,
        out_shape=jax.ShapeDtypeStruct((l, n), F32),
        grid=(l // tm, nt),
        in_specs=[pl.BlockSpec((tm, d), lambda i, j: (i, 0)),
                  pl.BlockSpec((1, d), lambda i, j: (0, 0)),
                  pl.BlockSpec((d, tn), lambda i, j: (0, j)),
                  pl.BlockSpec((d, tn), lambda i, j: (0, j + nt)),
                  pl.BlockSpec((1, tn), lambda i, j: (0, j)),
                  pl.BlockSpec((1, tn), lambda i, j: (0, j + nt))],
        out_specs=pl.BlockSpec((tm, tn), lambda i, j: (i, j)),
        scratch_shapes=[pltpu.VMEM((tm, d), BF16)],
        compiler_params=_params("parallel", "arbitrary"),
        name="norm_matmul_glu",
    )(x, _row(nw), w, w, b, b)


def _mm_res_kernel(a_ref, w_ref, r_ref, o_ref):
    o_ref[...] = r_ref[...] + _dot(a_ref[...], w_ref[...])


def _matmul_residual(a, w, res, *, tm, tn):
    l, k = a.shape
    n = w.shape[1]
    return pl.pallas_call(
        _mm_res_kernel,
        out_shape=jax.ShapeDtypeStruct((l, n), F32),
        grid=(l // tm, n // tn),
        in_specs=[pl.BlockSpec((tm, k), lambda i, j: (i, 0)),
                  pl.BlockSpec((k, tn), lambda i, j: (0, j)),
                  pl.BlockSpec((tm, tn), lambda i, j: (i, j))],
        out_specs=pl.BlockSpec((tm, tn), lambda i, j: (i, j)),
        compiler_params=_params("parallel", "arbitrary"),
        name="matmul_residual",
    )(a, w, res)


def _mlp_kernel(x_ref, nw_ref, wu_ref, wd_ref, o_ref, xn_ref):
    j = pl.program_id(1)

    @pl.when(j == 0)
    def _():
        _rms_norm_to(x_ref, nw_ref, xn_ref, RMS_EPS)

    h = jnp.maximum(_dot(xn_ref[...], wu_ref[...]), 0.0)
    contrib = _dot((h * h).astype(BF16), wd_ref[...])

    @pl.when(j == 0)
    def _():
        o_ref[...] = x_ref[...] + contrib

    @pl.when(j > 0)
    def _():
        o_ref[...] += contrib


def _mlp(x, nw, w_up, w_down, *, tm, th):
    l, d = x.shape
    hid = w_up.shape[1]
    return pl.pallas_call(
        _mlp_kernel,
        out_shape=jax.ShapeDtypeStruct((l, d), F32),
        grid=(l // tm, hid // th),
        in_specs=[pl.BlockSpec((tm, d), lambda i, j: (i, 0)),
                  pl.BlockSpec((1, d), lambda i, j: (0, 0)),
                  pl.BlockSpec((d, th), lambda i, j: (0, j)),
                  pl.BlockSpec((th, d), lambda i, j: (j, 0))],
        out_specs=pl.BlockSpec((tm, d), lambda i, j: (i, 0)),
        scratch_shapes=[pltpu.VMEM((tm, d), BF16)],
        compiler_params=_params("parallel", "arbitrary"),
        name="mlp",
    )(x, _row(nw), w_up, w_down)


def _sb_attn_kernel(q_ref, k_ref, v_ref, u_ref, o_ref, acc_ref, c_ref, z_ref, att_ref, *, tq, kb):
    i = pl.program_id(1)
    nsub = tq // kb
    reps = kb // V7X_LANES
    acc_ref[...] = jnp.zeros_like(acc_ref)
    c_ref[...] = jnp.zeros_like(c_ref)

    def scores(rows, start, nkeys):
        kk = k_ref[pl.ds(start, nkeys), :]
        return lax.dot_general(q_ref[rows, :], kk, (((1,), (1,)), ((), ())),
                               preferred_element_type=F32)

    def weights(z, rows, masked):
        soft = jnp.maximum(z, 0.0) + jnp.log(1.0 + jnp.exp2(-jnp.abs(z))) * LOG2_E
        log_beta = z - soft
        if masked:
            row_id = lax.broadcasted_iota(jnp.int32, z.shape, 0)
            col_id = lax.broadcasted_iota(jnp.int32, z.shape, 1)
            strict = col_id < row_id
            soft = jnp.where(strict, soft, 0.0)
        c = c_ref[rows, :]
        nkb = z.shape[1] // kb
        atts = [None] * nkb
        for m in reversed(range(nkb)):
            cols = slice(m * kb, (m + 1) * kb)
            sm = soft[:, cols]
            r = _dot(sm.astype(BF16), u_ref[...])
            a = jnp.exp2(log_beta[:, cols] + r + jnp.concatenate([c] * reps, axis=1))
            if masked:
                a = jnp.where(strict, a, 0.0)
            atts[m] = a.astype(BF16)
            c = c - jnp.sum(sm, axis=-1, keepdims=True)
        c_ref[rows, :] = c
        return atts[0] if nkb == 1 else jnp.concatenate(atts, axis=1)

    for m in reversed(range(nsub)):
        rows = slice(m * kb, tq)
        start = pl.multiple_of(i * tq + m * kb, kb)
        att = weights(scores(rows, start, kb), rows, True)
        acc_ref[rows, :] += _dot(att, v_ref[pl.ds(start, kb), :])

    def key_start(t):
        return pl.multiple_of(jnp.clip(i - 1 - t, 0, i) * tq, tq)

    rows = slice(0, tq)
    odd = i % 2

    @pl.when(odd == 1)
    def _():
        att = weights(scores(rows, key_start(0), tq), rows, False)
        acc_ref[...] += _dot(att, v_ref[pl.ds(key_start(0), tq), :])

    z_ref[0] = scores(rows, key_start(odd), tq)
    att_ref[1] = jnp.zeros(att_ref.shape[1:], att_ref.dtype)

    def pair(p, carry):
        t = odd + 2 * p
        for half in range(2):
            acc_ref[...] += _dot(att_ref[1 - half], v_ref[pl.ds(key_start(t + half - 1), tq), :])
            z = z_ref[half]
            z_ref[1 - half] = scores(rows, key_start(t + half + 1), tq)
            att_ref[half] = weights(z, rows, False)
        return carry
    lax.fori_loop(0, i // 2, pair, 0)

    last = _dot(att_ref[1], v_ref[pl.ds(0, tq), :])
    o_ref[...] = (acc_ref[...] + last).astype(o_ref.dtype)


def _sb_attention(qkv, *, heads, hd, tq):
    l = qkv.shape[0]
    kb = SB_BLOCK_KEYS
    j_id = lax.broadcasted_iota(jnp.int32, (kb, kb), 0)
    s_id = lax.broadcasted_iota(jnp.int32, (kb, kb), 1)
    u = jnp.where(j_id > s_id, -1.0, 0.0).astype(BF16)
    kern = functools.partial(_sb_attn_kernel, tq=tq, kb=kb)
    return pl.pallas_call(
        kern,
        out_shape=jax.ShapeDtypeStruct((l, heads * hd), BF16),
        grid=(heads, l // tq),
        in_specs=[pl.BlockSpec((tq, hd), lambda h, i: (i, h)),
                  pl.BlockSpec((l, hd), lambda h, i: (0, heads + h)),
                  pl.BlockSpec((l, hd), lambda h, i: (0, 2 * heads + h)),
                  pl.BlockSpec((kb, kb), lambda h, i: (0, 0))],
        out_specs=pl.BlockSpec((tq, hd), lambda h, i: (i, h)),
        scratch_shapes=[pltpu.VMEM((tq, hd), F32), pltpu.VMEM((tq, V7X_LANES), F32),
                        pltpu.VMEM((2, tq, tq), F32), pltpu.VMEM((2, tq, tq), BF16)],
        compiler_params=_params("parallel", "arbitrary"),
        name="sb_attention",
    )(qkv, qkv, qkv, u)


def _cf_conv_kernel(u_ref, halo_ref, dw_ref, dwb_ref, lnw_ref, lnb_ref, wo_ref, bo_ref, x_ref,
                    o_ref, ext_ref, y_ref, sh_ref, *, taps, halo, rows):
    i = pl.program_id(0)
    tl, ch = u_ref.shape

    @pl.when(i == 0)
    def _():
        ext_ref[0:halo, :] = jnp.zeros((halo, ch), F32)

    @pl.when(i > 0)
    def _():
        ext_ref[0:halo, :] = halo_ref[...]

    ext_ref[halo:, :] = u_ref[...]

    off = halo - (taps - 1)

    n_sh = tl + halo - 8

    def lane_block(c, carry):
        c0 = pl.multiple_of(c * V7X_LANES, V7X_LANES)
        lanes = pl.ds(c0, V7X_LANES)
        for s in range(1, 8):
            sh_ref[s, 0:n_sh, :] = ext_ref[s:s + n_sh, lanes]
        for r0 in range(0, tl, rows):
            acc = jnp.zeros((rows, V7X_LANES), F32) + dwb_ref[:, lanes]
            for k in range(taps):
                a8, s = (off + k) // 8 * 8, (off + k) % 8
                if s == 0:
                    win = ext_ref[r0 + a8:r0 + a8 + rows, lanes]
                else:
                    win = sh_ref[s, r0 + a8:r0 + a8 + rows, :]
                acc = acc + dw_ref[k:k + 1, lanes] * win
            y_ref[r0:r0 + rows, lanes] = acc
        return carry
    lax.fori_loop(0, ch // V7X_LANES, lane_block, 0)

    y = y_ref[...]
    mu = jnp.mean(y, axis=-1, keepdims=True)
    yc = y - mu
    var = jnp.mean(yc * yc, axis=-1, keepdims=True)
    v = yc * lax.rsqrt(var + LN_EPS) * lnw_ref[...] + lnb_ref[...]
    v = _silu(v).astype(BF16)
    o_ref[...] = x_ref[...] + _dot(v, wo_ref[...]) + bo_ref[...]


def _conformer_conv_out(u, dw_w, dw_b, ln_w, ln_b, w_out, b_out, x, *, tl):
    l, ch = u.shape
    taps = dw_w.shape[0]
    halo = 32
    assert taps - 1 <= halo and tl % halo == 0
    kern = functools.partial(_cf_conv_kernel, taps=taps, halo=halo, rows=64)
    hb = tl // halo
    return pl.pallas_call(
        kern,
        out_shape=jax.ShapeDtypeStruct((l, ch), F32),
        grid=(l // tl,),
        in_specs=[pl.BlockSpec((tl, ch), lambda i: (i, 0)),
                  pl.BlockSpec((halo, ch), lambda i: (jnp.maximum(i * hb - 1, 0), 0)),
                  pl.BlockSpec((taps, ch), lambda i: (0, 0)),
                  pl.BlockSpec((1, ch), lambda i: (0, 0)),
                  pl.BlockSpec((1, ch), lambda i: (0, 0)),
                  pl.BlockSpec((1, ch), lambda i: (0, 0)),
                  pl.BlockSpec((ch, ch), lambda i: (0, 0)),
                  pl.BlockSpec((1, ch), lambda i: (0, 0)),
                  pl.BlockSpec((tl, ch), lambda i: (i, 0))],
        out_specs=pl.BlockSpec((tl, ch), lambda i: (i, 0)),
        scratch_shapes=[pltpu.VMEM((tl + halo, ch), F32), pltpu.VMEM((tl, ch), F32),
                        pltpu.VMEM((8, tl + halo, V7X_LANES), F32)],
        compiler_params=_params("arbitrary"),
        name="conformer_conv_out",
    )(u, u, dw_w, _row(dw_b), _row(ln_w), _row(ln_b), w_out, _row(b_out), x)


def _ssd_kernel(zx_ref, halo_ref, cw_ref, cb_ref, dtb_ref, al_ref, e5_ref,
                d5_ref, nw_ref, tril_ref, o_ref, s_ref, ext_ref, *,
                groups, heads_per_group, hp, n, d_inner, conv_dim, taps, halo):
    ci = pl.program_id(0)
    q = zx_ref.shape[0]
    gw = heads_per_group * hp
    xbc0 = d_inner
    dt0 = d_inner + conv_dim

    @pl.when(ci == 0)
    def _():
        s_ref[...] = jnp.zeros_like(s_ref)
        ext_ref[0:halo, :] = jnp.zeros((halo, conv_dim), F32)

    @pl.when(ci > 0)
    def _():
        ext_ref[0:halo, :] = halo_ref[:, xbc0:dt0]

    ext_ref[halo:, :] = zx_ref[:, xbc0:dt0]
    off = halo - (taps - 1)

    def conv_silu(c0, width):
        cols = slice(c0, c0 + width)
        acc = cb_ref[:, cols] + cw_ref[0:1, cols] * ext_ref[off:off + q, cols]
        for k in range(1, taps):
            acc = acc + cw_ref[k:k + 1, cols] * ext_ref[off + k:off + k + q, cols]
        return _silu(acc)

    dt = _softplus(zx_ref[:, dt0:dt0 + V7X_LANES] + dtb_ref[...])
    da_hi, da_lo = _split_bf16(dt * (-jnp.exp(al_ref[...]) * LOG2_E))
    acum = _dot(tril_ref[...], da_hi) + _dot(tril_ref[...], da_lo)
    per_head = jnp.concatenate(_split_bf16(dt) + _split_bf16(acum), axis=0)

    row_id = lax.broadcasted_iota(jnp.int32, (q, q), 0)
    col_id = lax.broadcasted_iota(jnp.int32, (q, q), 1)
    causal = col_id <= row_id
    first_half = lax.broadcasted_iota(jnp.int32, (q, 2 * hp), 1) < hp

    for g in range(groups):
        wide = _dot(per_head, e5_ref[g])
        dt5 = wide[0:q] + wide[q:2 * q]
        acum5 = wide[2 * q:3 * q] + wide[3 * q:4 * q]

        xs = conv_silu(g * gw, gw)
        bm = conv_silu(d_inner + g * n, n)
        cm = conv_silu(d_inner + groups * n + g * n, n).astype(BF16)
        xdt = xs * dt5
        last = acum5[q - 1:q, :]
        xdec = (xdt * jnp.exp2(last - acum5)).astype(BF16)
        state = s_ref[g]

        y = _dot(cm, state.astype(BF16)) * jnp.exp2(acum5)
        s_ref[g] = state * jnp.exp2(last) + _dot(bm.T.astype(BF16), xdec)

        cb = lax.dot_general(cm, bm.astype(BF16), (((1,), (1,)), ((), ())),
                             preferred_element_type=F32)
        diag = []
        for pair in range(heads_per_group // 2):
            lanes = slice(pair * 2 * hp, (pair + 1) * 2 * hp)
            xpair = xdt[:, lanes]
            cpair = acum5[:, lanes]
            swapped = pltpu.roll(cpair, hp, axis=1)
            cpair_t = cpair.T
            ypair = jnp.zeros((q, 2 * hp), F32)
            for half in range(2):
                col = jnp.where(first_half, cpair, swapped) if half == 0 else \
                    jnp.where(first_half, swapped, cpair)
                seg = col - cpair_t[half * hp:half * hp + 1, :]
                m = (cb * jnp.exp2(jnp.where(causal, seg, -jnp.inf))).astype(BF16)
                mine = first_half if half == 0 else jnp.logical_not(first_half)
                ypair = ypair + _dot(m, jnp.where(mine, xpair, 0.0).astype(BF16))
            diag.append(ypair)
        y = y + jnp.concatenate(diag, axis=1) + d5_ref[g] * xs

        y = y * _silu(zx_ref[:, g * gw:(g + 1) * gw])
        ms = jnp.mean(y * y, axis=-1, keepdims=True)
        o_ref[:, g * gw:(g + 1) * gw] = (y * lax.rsqrt(ms + M2_NORM_EPS) * nw_ref[g]).astype(o_ref.dtype)


def _ssd(zx, conv_w, conv_b, dt_bias, a_log, d_skip, norm_w, *, d_inner, heads):
    l, zw = zx.shape
    g = M2_GROUPS
    n = M2_STATE
    qn = M2_CHUNK
    hpg = heads // g
    hp = d_inner // heads
    gw = hpg * hp
    taps, conv_dim = conv_w.shape
    halo = 8
    assert qn == V7X_LANES and n == V7X_LANES and 2 * hp == V7X_LANES and heads <= V7X_LANES
    assert zw == d_inner + conv_dim + V7X_LANES and taps - 1 <= halo

    hid = lax.broadcasted_iota(jnp.int32, (g, V7X_LANES, gw), 1)
    gid = lax.broadcasted_iota(jnp.int32, (g, V7X_LANES, gw), 0)
    lid = lax.broadcasted_iota(jnp.int32, (g, V7X_LANES, gw), 2)
    e5 = (hid == gid * hpg + lid // hp).astype(BF16)
    tril = (lax.broadcasted_iota(jnp.int32, (qn, qn), 1)
            <= lax.broadcasted_iota(jnp.int32, (qn, qn), 0)).astype(BF16)

    dtb = jnp.zeros((1, V7X_LANES), F32).at[0, :heads].set(dt_bias)
    al = jnp.zeros((1, V7X_LANES), F32).at[0, :heads].set(a_log)
    d5 = jnp.repeat(d_skip, hp).reshape(g, 1, gw)
    nw = norm_w.reshape(g, 1, gw)

    kern = functools.partial(_ssd_kernel, groups=g, heads_per_group=hpg, hp=hp, n=n,
                             d_inner=d_inner, conv_dim=conv_dim, taps=taps, halo=halo)
    hb = qn // halo
    const2 = lambda ci: (0, 0)
    const3 = lambda ci: (0, 0, 0)
    return pl.pallas_call(
        kern,
        out_shape=jax.ShapeDtypeStruct((l, d_inner), BF16),
        grid=(l // qn,),
        in_specs=[pl.BlockSpec((qn, zw), lambda ci: (ci, 0)),
                  pl.BlockSpec((halo, zw), lambda ci: (jnp.maximum(ci * hb - 1, 0), 0)),
                  pl.BlockSpec((taps, conv_dim), const2),
                  pl.BlockSpec((1, conv_dim), const2),
                  pl.BlockSpec((1, V7X_LANES), const2),
                  pl.BlockSpec((1, V7X_LANES), const2),
                  pl.BlockSpec((g, V7X_LANES, gw), const3),
                  pl.BlockSpec((g, 1, gw), const3),
                  pl.BlockSpec((g, 1, gw), const3),
                  pl.BlockSpec((qn, qn), const2)],
        out_specs=pl.BlockSpec((qn, d_inner), lambda ci: (ci, 0)),
        scratch_shapes=[pltpu.VMEM((g, n, gw), F32), pltpu.VMEM((qn + halo, conv_dim), F32)],
        compiler_params=_params("arbitrary"),
        name="ssd",
    )(zx, zx, conv_w, _row(conv_b), dtb, al, e5, d5, nw, tril)


def _tiles():
    return dict(tm=512, tn=1024, tn_wide=2048, th=1024, tq=512, tl_cf=256)


def _stick_breaking_layer(x, nw, w_qkv, q_gain, k_gain, w_o, t):
    d = x.shape[1]
    hd = q_gain.shape[0]
    heads = d // hd
    scale = LOG2_E / math.sqrt(hd)
    gains = jnp.concatenate([jnp.tile(q_gain * scale, heads), jnp.tile(k_gain, heads),
                             jnp.ones((d,), F32)])
    qkv = _norm_matmul_qkv(x, nw, w_qkv, gains, n_norm_cols=2 * d, hd=hd,
                           tm=t["tm"], tn=t["tn_wide"])
    o = _sb_attention(qkv, heads=heads, hd=hd, tq=t["tq"])
    return _matmul_residual(o, w_o, x, tm=t["tm"], tn=t["tn_wide"])


def _conformer_layer(x, nw, w_in, b_in, dw_w, dw_b, ln_w, ln_b, w_out, b_out, t):
    u = _norm_matmul_glu(x, nw, w_in, b_in, tm=t["tm"], tn=t["tn"])
    return _conformer_conv_out(u, dw_w, dw_b, ln_w, ln_b, w_out, b_out, x, tl=t["tl_cf"])


def _mamba2_layer(x, nw, w_in, conv_w, conv_b, dt_bias, a_log, d_skip, norm_w, w_out, t):
    d_inner = norm_w.shape[0]
    heads = a_log.shape[0]
    n_in = w_in.shape[1]
    n_pad = -(-n_in // (9 * V7X_LANES)) * (9 * V7X_LANES)
    w_pad = jnp.pad(w_in, ((0, 0), (0, n_pad - n_in)))
    zx = _norm_matmul(x, nw, w_pad, tm=t["tm"], tn=9 * V7X_LANES, out_dtype=F32)
    y = _ssd(zx, conv_w, conv_b, dt_bias, a_log, d_skip, norm_w, d_inner=d_inner, heads=heads)
    return _matmul_residual(y, w_out, x, tm=t["tm"], tn=t["tn"])


def kernel(x, norm_mix_w, norm_mlp_w, sb_w_qkv, sb_q_norm_w, sb_k_norm_w, sb_w_o, cf_w_in, cf_b_in, cf_dw_w, cf_dw_b, cf_ln_w, cf_ln_b, cf_w_out, cf_b_out, m2_w_in, m2_conv_w, m2_conv_b, m2_dt_bias, m2_a_log, m2_d, m2_norm_w, m2_w_out, mlp_w_up, mlp_w_down):
    b, l, d = x.shape
    depth = norm_mix_w.shape[0]
    t = _tiles()
    (sb_w_qkv, sb_w_o, cf_w_in, cf_w_out, m2_w_in, m2_w_out, mlp_w_up, mlp_w_down) = (
        w.astype(BF16) for w in (sb_w_qkv, sb_w_o, cf_w_in, cf_w_out, m2_w_in, m2_w_out,
                                 mlp_w_up, mlp_w_down))
    outs = []
    for bi in range(b):
        h = x[bi]
        for i in range(depth):
            kind, j = i % N_MIXERS, i // N_MIXERS
            if kind == 0:
                h = _stick_breaking_layer(h, norm_mix_w[i], sb_w_qkv[j], sb_q_norm_w[j],
                                          sb_k_norm_w[j], sb_w_o[j], t)
            elif kind == 1:
                h = _conformer_layer(h, norm_mix_w[i], cf_w_in[j], cf_b_in[j], cf_dw_w[j],
                                     cf_dw_b[j], cf_ln_w[j], cf_ln_b[j], cf_w_out[j],
                                     cf_b_out[j], t)
            else:
                h = _mamba2_layer(h, norm_mix_w[i], m2_w_in[j], m2_conv_w[j], m2_conv_b[j],
                                  m2_dt_bias[j], m2_a_log[j], m2_d[j], m2_norm_w[j],
                                  m2_w_out[j], t)
            h = _mlp(h, norm_mlp_w[i], mlp_w_up[i], mlp_w_down[i], tm=t["tm"], th=t["th"])
        outs.append(h)
    return jnp.stack(outs)
```
